```python
import jax, jax.numpy as jnp
from jax import lax
import numpy as np

D_MODEL = 1024
BATCH = 8
SEQ = 4096
DEPTH = 2

N_EVEN = (DEPTH + 1) // 2
N_ODD = DEPTH // 2
EPS = 1e-6

D_RNN = D_MODEL // 2
RNN_BLOCKS = 8
RNN_BLOCK_W = D_RNN // RNN_BLOCKS
CONV_W = 4
RGLRU_C = 8.0
FOX_HEADS = 8
FOX_HEAD_DIM = 64
D_FOX = FOX_HEADS * FOX_HEAD_DIM
Q_BLOCK = 128
EVEN_PROJ = 2 * D_RNN + 3 * D_FOX + FOX_HEADS
EVEN_MIX = D_RNN + D_FOX
GLA_HEADS = 4
GLA_DK = D_MODEL // 2
GLA_DV = D_MODEL
GLA_HEAD_DK = GLA_DK // GLA_HEADS
GLA_HEAD_DV = GLA_DV // GLA_HEADS
GATE_RANK = 16
GATE_TAU = 16.0
GLA_CHUNK = 64
ODD_PROJ = 2 * GLA_DK + 2 * GLA_DV + GATE_RANK
N_EXPERTS = 32
N_GROUPS = 4
EXPERTS_PER_GROUP = N_EXPERTS // N_GROUPS
TOP_K = 2
D_EXPERT = D_MODEL // 2
ROUTE_BLOCK = 128

kernel_name = 'hybrid_rglru_fox_gla_grouped_moe_adaln'

F32 = jnp.float32


def rms_norm(x, gain):
    xf = x.astype(F32)
    xf = xf * lax.rsqrt(jnp.mean(xf * xf, axis=-1, keepdims=True) + EPS)
    return (xf * gain.astype(F32)).astype(x.dtype)


def causal_depthwise_conv(x, w, bias):
    ch = x.shape[-1]
    y = lax.conv_general_dilated(x, w[:, None, :], window_strides=(1,), padding=[(CONV_W - 1, 0)],
                                 dimension_numbers=('NWC', 'WIO', 'NWC'), feature_group_count=ch)
    return y + bias


def _linear_combine(left, right):
    a1, b1 = left
    a2, b2 = right
    return a1 * a2, a2 * b1 + b2


def rg_lru(x, r_w, r_b, i_w, i_b, lam):
    b, s, _ = x.shape
    xb = x.reshape(b, s, RNN_BLOCKS, RNN_BLOCK_W)
    r = jax.nn.sigmoid(jnp.einsum('bsnd,nde->bsne', xb, r_w.astype(F32)) + r_b.astype(F32)).reshape(b, s, D_RNN)
    gi = jax.nn.sigmoid(jnp.einsum('bsnd,nde->bsne', xb, i_w.astype(F32)) + i_b.astype(F32)).reshape(b, s, D_RNN)
    log_a = -RGLRU_C * jax.nn.softplus(-lam.astype(F32)) * r
    a = jnp.exp(log_a)
    u = jnp.sqrt(-jnp.expm1(2.0 * log_a)) * (gi * x)
    _, h = lax.associative_scan(_linear_combine, (a, u), axis=1)
    return h


def fox_attention(q, k, v, log_f):
    b, h, s, hd = q.shape
    n_blk = s // Q_BLOCK
    cum = jnp.cumsum(log_f, axis=-1)
    scale = hd ** -0.5
    q_blocks = jnp.moveaxis(q.reshape(b, h, n_blk, Q_BLOCK, hd), 2, 0)
    c_blocks = jnp.moveaxis(cum.reshape(b, h, n_blk, Q_BLOCK), 2, 0)
    k_pos = jnp.arange(s)

    def one_block(args):
        idx, qb, cb = args
        logits = jnp.einsum('bhqd,bhkd->bhqk', qb, k) * scale
        logits = logits + (cb[..., :, None] - cum[..., None, :])
        q_pos = idx * Q_BLOCK + jnp.arange(Q_BLOCK)
        causal = k_pos[None, :] <= q_pos[:, None]
        p = jax.nn.softmax(jnp.where(causal, logits, -jnp.inf), axis=-1)
        return jnp.einsum('bhqk,bhkd->bhqd', p, v)

    out = lax.map(one_block, (jnp.arange(n_blk), q_blocks, c_blocks))
    return jnp.moveaxis(out, 0, 2).reshape(b, h, s, hd)


def gla_chunked(q, k, v, log_g):
    b, h, s, dk = q.shape
    dv = v.shape[-1]
    n = s // GLA_CHUNK
    q = q.reshape(b, h, n, GLA_CHUNK, dk) * (dk ** -0.5)
    k = k.reshape(b, h, n, GLA_CHUNK, dk)
    v = v.reshape(b, h, n, GLA_CHUNK, dv)
    cum = jnp.cumsum(log_g.reshape(b, h, n, GLA_CHUNK, dk), axis=3)
    last = cum[:, :, :, -1:, :]
    q_t = q * jnp.exp(cum)
    k_t = k * jnp.exp(-cum)
    k_end = k * jnp.exp(last - cum)
    mask = jnp.tril(jnp.ones((GLA_CHUNK, GLA_CHUNK), dtype=bool))
    intra = jnp.where(mask, jnp.einsum('bhnqd,bhnkd->bhnqk', q_t, k_t), 0.0)
    o_intra = jnp.einsum('bhnqk,bhnkv->bhnqv', intra, v)
    kv = jnp.einsum('bhnkd,bhnkv->bhndv', k_end, v)
    decay = jnp.exp(last[:, :, :, 0, :])

    def step(state, inp):
        dec, kv_n = inp
        return state * dec[..., None] + kv_n, state

    _, states = lax.scan(step, jnp.zeros((b, h, dk, dv), F32), (jnp.moveaxis(decay, 2, 0), jnp.moveaxis(kv, 2, 0)))
    states = jnp.moveaxis(states, 0, 2)
    o_inter = jnp.einsum('bhnqd,bhndv->bhnqv', q_t, states)
    return (o_intra + o_inter).reshape(b, h, s, dv)


def even_mixer(h, w_in, conv_w, conv_b, r_w, r_b, i_w, i_b, lam, forget_b, q_gain, k_gain, w_out):
    b, s, _ = h.shape
    proj = (h @ w_in).astype(F32)
    splits = np.cumsum([D_RNN, D_RNN, D_FOX, D_FOX, D_FOX]).tolist()
    x_rnn, gate_rnn, q, k, v, f_logit = jnp.split(proj, splits, axis=-1)
    x_rnn = causal_depthwise_conv(x_rnn, conv_w.astype(F32), conv_b.astype(F32))
    y_rnn = rg_lru(x_rnn, r_w, r_b, i_w, i_b, lam) * jax.nn.gelu(gate_rnn)
    def heads(t):
        return t.reshape(b, s, FOX_HEADS, FOX_HEAD_DIM).transpose(0, 2, 1, 3)
    qh = rms_norm(heads(q), q_gain)
    kh = rms_norm(heads(k), k_gain)
    log_f = jax.nn.log_sigmoid(f_logit + forget_b.astype(F32)).transpose(0, 2, 1)
    y_fox = fox_attention(qh, kh, heads(v), log_f).transpose(0, 2, 1, 3).reshape(b, s, D_FOX)
    y = jnp.concatenate([y_rnn, y_fox], axis=-1).astype(h.dtype)
    return y @ w_out


def odd_mixer(h, w_in, gate_w2, gate_b, out_gain, w_out):
    b, s, _ = h.shape
    proj = (h @ w_in).astype(F32)
    splits = np.cumsum([GLA_DK, GLA_DK, GLA_DV, GLA_DV]).tolist()
    q, k, v, r, g_low = jnp.split(proj, splits, axis=-1)
    log_g = jax.nn.log_sigmoid(g_low @ gate_w2.astype(F32) + gate_b.astype(F32)) / GATE_TAU
    def heads(t, d):
        return t.reshape(b, s, GLA_HEADS, d).transpose(0, 2, 1, 3)
    o = gla_chunked(heads(q, GLA_HEAD_DK), heads(k, GLA_HEAD_DK), heads(v, GLA_HEAD_DV), heads(log_g, GLA_HEAD_DK))
    o = rms_norm(o, out_gain).transpose(0, 2, 1, 3).reshape(b, s, GLA_DV)
    y = (o * jax.nn.silu(r)).astype(h.dtype)
    return y @ w_out


def moe_ffn(h, router_w, router_b, w1, w3, w2):
    b, s, d = h.shape
    t = h.reshape(-1, d)
    n_tok = t.shape[0]
    scores = jax.nn.sigmoid(t.astype(F32) @ router_w.astype(F32))
    sel = (scores + router_b.astype(F32)).reshape(n_tok, N_GROUPS, EXPERTS_PER_GROUP)
    group_score = lax.top_k(sel, 2)[0].sum(-1)
    g_idx = jnp.argmax(group_score, axis=-1)
    in_group = jnp.take_along_axis(sel, g_idx[:, None, None], axis=1)[:, 0]
    _, local = lax.top_k(in_group, TOP_K)
    e_idx = g_idx[:, None] * EXPERTS_PER_GROUP + local
    gw = jnp.take_along_axis(scores, e_idx, axis=-1)
    gw = gw / jnp.sum(gw, axis=-1, keepdims=True)
    m = n_tok * TOP_K
    flat_e = e_idx.reshape(-1)
    flat_tok = jnp.repeat(jnp.arange(n_tok), TOP_K)
    flat_w = gw.reshape(-1)
    order = jnp.argsort(flat_e)
    e_sorted = flat_e[order]
    tok_sorted = flat_tok[order]
    counts = jnp.bincount(flat_e, length=N_EXPERTS)
    starts = jnp.cumsum(counts) - counts
    padded = (counts + ROUTE_BLOCK - 1) // ROUTE_BLOCK * ROUTE_BLOCK
    pend = jnp.cumsum(padded)
    pstarts = pend - padded
    dest = pstarts[e_sorted] + (jnp.arange(m) - starts[e_sorted])
    n_blocks = -(-m // ROUTE_BLOCK) + N_EXPERTS
    p_len = n_blocks * ROUTE_BLOCK
    x_pad = jnp.zeros((p_len, d), t.dtype).at[dest].set(t[tok_sorted])
    blk_e = jnp.minimum(jnp.searchsorted(pend, jnp.arange(n_blocks) * ROUTE_BLOCK, side='right'), N_EXPERTS - 1)

    def run_block(args):
        xb, e = args
        hid = jax.nn.silu(xb @ w1[e]) * (xb @ w3[e])
        return hid @ w2[e]

    y_pad = lax.map(run_block, (x_pad.reshape(n_blocks, ROUTE_BLOCK, d), blk_e)).reshape(p_len, d)
    y_sorted = y_pad[dest].astype(F32) * flat_w[order][:, None]
    out = jax.ops.segment_sum(y_sorted, tok_sorted, num_segments=n_tok)
    return out.astype(h.dtype).reshape(b, s, d)


def setup_inputs(seed: int = 0) -> dict:
    key = jax.random.key(seed)
    ks = iter(jax.random.split(key, 64))

    def nrm(shape, scale):
        return jax.random.normal(next(ks), shape, F32) * scale

    u = jax.random.uniform(next(ks), (N_EVEN, D_RNN), F32, 0.9, 0.999)
    a0 = u ** (1.0 / RGLRU_C)
    ev_lam = jnp.log(a0) - jnp.log1p(-a0)
    ev_forget_b = jnp.broadcast_to(jnp.linspace(1.0, 5.0, FOX_HEADS, dtype=F32), (N_EVEN, FOX_HEADS)) + nrm((N_EVEN, FOX_HEADS), 0.1)
    return {
        'x': nrm((BATCH, SEQ, D_MODEL), 1.0),
        'c': nrm((BATCH, D_MODEL), 1.0),
        'router_w': nrm((D_MODEL, N_EXPERTS), D_MODEL ** -0.5),
        'router_b': nrm((N_EXPERTS,), 0.01),
        'ada_w': nrm((DEPTH, D_MODEL, 6 * D_MODEL), 0.5 * D_MODEL ** -0.5),
        'ada_b': nrm((DEPTH, 6 * D_MODEL), 0.02),
        'norm_mix': 1.0 + nrm((DEPTH, D_MODEL), 0.02),
        'norm_ffn': 1.0 + nrm((DEPTH, D_MODEL), 0.02),
        'ev_w_in': nrm((N_EVEN, D_MODEL, EVEN_PROJ), D_MODEL ** -0.5),
        'ev_conv_w': nrm((N_EVEN, CONV_W, D_RNN), 0.5),
        'ev_conv_b': nrm((N_EVEN, D_RNN), 0.02),
        'ev_r_w': nrm((N_EVEN, RNN_BLOCKS, RNN_BLOCK_W, RNN_BLOCK_W), RNN_BLOCK_W ** -0.5),
        'ev_r_b': nrm((N_EVEN, RNN_BLOCKS, RNN_BLOCK_W), 0.02),
        'ev_i_w': nrm((N_EVEN, RNN_BLOCKS, RNN_BLOCK_W, RNN_BLOCK_W), RNN_BLOCK_W ** -0.5),
        'ev_i_b': nrm((N_EVEN, RNN_BLOCKS, RNN_BLOCK_W), 0.02),
        'ev_lam': ev_lam,
        'ev_forget_b': ev_forget_b,
        'ev_q_gain': 1.0 + nrm((N_EVEN, FOX_HEAD_DIM), 0.02),
        'ev_k_gain': 1.0 + nrm((N_EVEN, FOX_HEAD_DIM), 0.02),
        'ev_w_out': nrm((N_EVEN, EVEN_MIX, D_MODEL), EVEN_MIX ** -0.5),
        'od_w_in': nrm((N_ODD, D_MODEL, ODD_PROJ), D_MODEL ** -0.5),
        'od_gate_w2': nrm((N_ODD, GATE_RANK, GLA_DK), GATE_RANK ** -0.5),
        'od_gate_b': nrm((N_ODD, GLA_DK), 0.02),
        'od_out_gain': 1.0 + nrm((N_ODD, GLA_HEAD_DV), 0.02),
        'od_w_out': nrm((N_ODD, GLA_DV, D_MODEL), GLA_DV ** -0.5),
        'moe_w1': nrm((DEPTH, N_EXPERTS, D_MODEL, D_EXPERT), D_MODEL ** -0.5),
        'moe_w3': nrm((DEPTH, N_EXPERTS, D_MODEL, D_EXPERT), D_MODEL ** -0.5),
        'moe_w2': nrm((DEPTH, N_EXPERTS, D_EXPERT, D_MODEL), D_EXPERT ** -0.5),
    }


def reference(x, c, router_w, router_b, ada_w, ada_b, norm_mix, norm_ffn,
              ev_w_in, ev_conv_w, ev_conv_b, ev_r_w, ev_r_b, ev_i_w, ev_i_b, ev_lam, ev_forget_b,
              ev_q_gain, ev_k_gain, ev_w_out,
              od_w_in, od_gate_w2, od_gate_b, od_out_gain, od_w_out,
              moe_w1, moe_w3, moe_w2):
    for layer in range(DEPTH):
        mod = (jax.nn.silu(c) @ ada_w[layer] + ada_b[layer])[:, None, :]
        sh_m, sc_m, g_m, sh_f, sc_f, g_f = jnp.split(mod, 6, axis=-1)
        h = rms_norm(x, norm_mix[layer]) * (1.0 + sc_m) + sh_m
        j = layer // 2
        if layer % 2 == 0:
            y = even_mixer(h, ev_w_in[j], ev_conv_w[j], ev_conv_b[j], ev_r_w[j], ev_r_b[j], ev_i_w[j], ev_i_b[j],
                           ev_lam[j], ev_forget_b[j], ev_q_gain[j], ev_k_gain[j], ev_w_out[j])
        else:
            y = odd_mixer(h, od_w_in[j], od_gate_w2[j], od_gate_b[j], od_out_gain[j], od_w_out[j])
        x = (x + g_m * y).astype(x.dtype)
        h = rms_norm(x, norm_ffn[layer]) * (1.0 + sc_f) + sh_f
        x = (x + g_f * moe_ffn(h, router_w, router_b, moe_w1[layer], moe_w3[layer], moe_w2[layer])).astype(x.dtype)
    return x
```

```python
import functools

import jax
import jax.numpy as jnp
from jax import lax
from jax.experimental import pallas as pl
from jax.experimental.pallas import tpu as pltpu

F32 = jnp.float32
BF16 = jnp.bfloat16
I32 = jnp.int32
HIGHEST = lax.Precision.HIGHEST

EPS = 1e-6
RGLRU_C = 8.0
CONV_W = 4
GATE_TAU = 16.0
GLA_CHUNK = 64
N_GROUPS = 4
TOP_K = 2
NEG_BIG = -1e30

LANES = 128
SUBLANES = 8
TOKEN_TILE = 512
SEQ_TILE = 256
ATTN_TILE = 256
EXPERT_ROWS = 256
VMEM_LIMIT = 48 * 1024 * 1024


def _params(*sem):
    return pltpu.CompilerParams(dimension_semantics=sem, vmem_limit_bytes=VMEM_LIMIT)


def _rms_mod(x, gain, sc, sh):
    ms = jnp.mean(x * x, axis=-1, keepdims=True)
    return (x * lax.rsqrt(ms + EPS) * gain) * (1.0 + sc) + sh


def _softplus(z):
    return jnp.maximum(z, 0.0) + jnp.log1p(jnp.exp(-jnp.abs(z)))


def _sigmoid(z):
    return 1.0 / (1.0 + jnp.exp(-z))


def _shift_rows(v, d, fill, row):
    return jnp.where(row >= d, pltpu.roll(v, d, axis=0), fill)


def _cumsum_rows(v):
    n = v.shape[0]
    row = lax.broadcasted_iota(I32, v.shape, 0)
    d = 1
    while d < n:
        v = v + _shift_rows(v, d, 0.0, row)
        d *= 2
    return v


def _ada_kernel(c_ref, w_ref, b_ref, o_ref):
    c = c_ref[...]
    o_ref[0] = jnp.dot(c * _sigmoid(c), w_ref[0], preferred_element_type=F32, precision=HIGHEST) + b_ref[0]


def _ada(c, ada_w, ada_b):
    depth, d, n = ada_w.shape
    b = c.shape[0]
    tn = n // 4
    return pl.pallas_call(
        _ada_kernel,
        grid=(depth, n // tn),
        in_specs=[pl.BlockSpec((b, d), lambda l, j: (0, 0)),
                  pl.BlockSpec((1, d, tn), lambda l, j: (l, 0, j)),
                  pl.BlockSpec((1, 1, tn), lambda l, j: (l, 0, j))],
        out_specs=pl.BlockSpec((1, b, tn), lambda l, j: (l, 0, j)),
        out_shape=jax.ShapeDtypeStruct((depth, b, n), F32),
        compiler_params=_params("parallel", "parallel"),
        name="ada_mod",
    )(c, ada_w, ada_b.reshape(depth, 1, n))


def _in_proj_kernel(x_ref, gain_ref, sc_ref, sh_ref, w_ref, wx_ref, o_ref, ox_ref, *, n_chunk):
    h = _rms_mod(x_ref[...], gain_ref[...], sc_ref[0], sh_ref[0]).astype(BF16)
    for c in range(0, o_ref.shape[1], n_chunk):
        o_ref[:, c:c + n_chunk] = jnp.dot(h, w_ref[:, c:c + n_chunk],
                                          preferred_element_type=F32).astype(o_ref.dtype)
    ox_ref[...] = jnp.dot(h, wx_ref[...], preferred_element_type=F32)


def _in_proj(x2, gain, sc, sh, w, wx, seq):
    t, d = x2.shape
    n = w.shape[1]
    tps = seq // TOKEN_TILE
    return pl.pallas_call(
        functools.partial(_in_proj_kernel, n_chunk=512),
        grid=(t // TOKEN_TILE,),
        in_specs=[pl.BlockSpec((TOKEN_TILE, d), lambda i: (i, 0)),
                  pl.BlockSpec((1, d), lambda i: (0, 0)),
                  pl.BlockSpec((1, 1, d), lambda i: (i // tps, 0, 0)),
                  pl.BlockSpec((1, 1, d), lambda i: (i // tps, 0, 0)),
                  pl.BlockSpec((d, n), lambda i: (0, 0)),
                  pl.BlockSpec((d, LANES), lambda i: (0, 0))],
        out_specs=[pl.BlockSpec((TOKEN_TILE, n), lambda i: (i, 0)),
                   pl.BlockSpec((TOKEN_TILE, LANES), lambda i: (i, 0))],
        out_shape=[jax.ShapeDtypeStruct((t, n), BF16), jax.ShapeDtypeStruct((t, LANES), F32)],
        compiler_params=_params("parallel"),
        name="in_proj",
    )(x2, gain, sc, sh, w, wx)


def _out_proj_kernel(*refs, n_parts):
    y_refs, w_refs = refs[:n_parts], refs[n_parts:2 * n_parts]
    x_ref, g_ref, o_ref = refs[2 * n_parts:]
    acc = jnp.dot(y_refs[0][...], w_refs[0][...], preferred_element_type=F32)
    for y_ref, w_ref in zip(y_refs[1:], w_refs[1:]):
        acc = acc + jnp.dot(y_ref[...], w_ref[...], preferred_element_type=F32)
    o_ref[...] = x_ref[...] + g_ref[0] * acc


def _out_proj(ys, ws, x2, g, seq):
    t, d = x2.shape
    tps = seq // TOKEN_TILE
    n_parts = len(ys)
    in_specs = [pl.BlockSpec((TOKEN_TILE, y.shape[1]), lambda i: (i, 0)) for y in ys]
    in_specs += [pl.BlockSpec(w.shape, lambda i: (0, 0)) for w in ws]
    in_specs += [pl.BlockSpec((TOKEN_TILE, d), lambda i: (i, 0)),
                 pl.BlockSpec((1, 1, d), lambda i: (i // tps, 0, 0))]
    return pl.pallas_call(
        functools.partial(_out_proj_kernel, n_parts=n_parts),
        grid=(t // TOKEN_TILE,),
        in_specs=in_specs,
        out_specs=pl.BlockSpec((TOKEN_TILE, d), lambda i: (i, 0)),
        out_shape=jax.ShapeDtypeStruct((t, d), F32),
        compiler_params=_params("parallel"),
        name="out_proj",
    )(*ys, *ws, x2, g)


def _rglru_kernel(x_ref, gate_ref, cw_ref, cb_ref, wg_ref, bg_ref, lam_ref, y_ref, ext_ref, hc_ref):
    rows, ch = x_ref.shape

    @pl.when(pl.program_id(1) == 0)
    def _():
        ext_ref[0:SUBLANES, :] = jnp.zeros((SUBLANES, ch), F32)
        hc_ref[...] = jnp.zeros_like(hc_ref)

    x = x_ref[...].astype(F32)
    ext_ref[SUBLANES:SUBLANES + rows, :] = x
    xc = cb_ref[...] + cw_ref[CONV_W - 1:CONV_W, :] * x
    for j in range(CONV_W - 1):
        off = SUBLANES - (CONV_W - 1) + j
        xc = xc + cw_ref[j:j + 1, :] * ext_ref[off:off + rows, :]
    ext_ref[0:SUBLANES, :] = x[rows - SUBLANES:, :]

    gates = jnp.dot(xc.astype(BF16), wg_ref[...], preferred_element_type=F32) + bg_ref[...]
    r = _sigmoid(gates[:, :ch])
    gi = _sigmoid(gates[:, ch:])
    log_a = (-RGLRU_C * _softplus(-lam_ref[...])) * r
    a = jnp.exp(log_a)
    u = jnp.sqrt(1.0 - a * a) * (gi * xc)

    row = lax.broadcasted_iota(I32, (rows, ch), 0)
    d = 1
    while d < rows:
        u = u + a * _shift_rows(u, d, 0.0, row)
        a = a * _shift_rows(a, d, 1.0, row)
        d *= 2
    h = u + a * hc_ref[...]
    hc_ref[...] = h[rows - 1:rows, :]
    y_ref[...] = (h * jax.nn.gelu(gate_ref[...].astype(F32))).astype(y_ref.dtype)


def _rglru(proj, conv_w, conv_b, wg, bg, lam, batch, seq):
    t = proj.shape[0]
    ch = conv_w.shape[1]
    nt = seq // SEQ_TILE
    return pl.pallas_call(
        _rglru_kernel,
        grid=(batch, nt),
        in_specs=[pl.BlockSpec((SEQ_TILE, ch), lambda b, j: (b * nt + j, 0)),
                  pl.BlockSpec((SEQ_TILE, ch), lambda b, j: (b * nt + j, 1)),
                  pl.BlockSpec((CONV_W, ch), lambda b, j: (0, 0)),
                  pl.BlockSpec((1, ch), lambda b, j: (0, 0)),
                  pl.BlockSpec((ch, 2 * ch), lambda b, j: (0, 0)),
                  pl.BlockSpec((1, 2 * ch), lambda b, j: (0, 0)),
                  pl.BlockSpec((1, ch), lambda b, j: (0, 0))],
        out_specs=pl.BlockSpec((SEQ_TILE, ch), lambda b, j: (b * nt + j, 0)),
        out_shape=jax.ShapeDtypeStruct((t, ch), BF16),
        scratch_shapes=[pltpu.VMEM((SEQ_TILE + SUBLANES, ch), F32), pltpu.VMEM((1, ch), F32)],
        compiler_params=_params("parallel", "arbitrary"),
        name="rglru",
    )(proj, proj, conv_w, conv_b, wg, bg, lam)


def _fox_prep_kernel(q_ref, k_ref, fx_ref, fb_ref, qg_ref, kg_ref, ones_ref, qn_ref, kn_ref, f_ref, ft_ref,
                     fc_ref, *, scale):
    @pl.when(pl.program_id(1) == 0)
    def _():
        fc_ref[...] = jnp.zeros_like(fc_ref)

    def qk_norm(v_ref, g_ref, mult):
        v = v_ref[...].astype(F32)
        ms = jnp.dot((v * v).astype(BF16), ones_ref[...], preferred_element_type=F32)
        return (v * lax.rsqrt(ms + EPS) * g_ref[...]) * mult

    qn_ref[...] = qk_norm(q_ref, qg_ref, scale).astype(qn_ref.dtype)
    kn_ref[...] = qk_norm(k_ref, kg_ref, 1.0).astype(kn_ref.dtype)

    log_f = -_softplus(-(fx_ref[...] + fb_ref[...]))
    f = _cumsum_rows(log_f) + fc_ref[...]
    fc_ref[...] = f[f.shape[0] - 1:, :]
    f_ref[...] = f
    ft_ref[0, 0] = f.T[:SUBLANES, :]


def _fox_prep(proj, fx, forget_b, q_gain, k_gain, batch, seq, heads, head_dim):
    t = proj.shape[0]
    dq = heads * head_dim
    nt = seq // ATTN_TILE
    qblk = (proj.shape[1] - 3 * dq) // dq
    fb = jnp.zeros((1, LANES), F32).at[0, :heads].set(forget_b)
    qg = jnp.tile(q_gain, heads)[None, :]
    kg = jnp.tile(k_gain, heads)[None, :]
    head_of = jnp.arange(dq) // head_dim
    ones = (head_of[:, None] == head_of[None, :]).astype(BF16) * (1.0 / head_dim)
    tok = lambda b, j: (b * nt + j, 0)
    return pl.pallas_call(
        functools.partial(_fox_prep_kernel, scale=head_dim ** -0.5),
        grid=(batch, nt),
        in_specs=[pl.BlockSpec((ATTN_TILE, dq), lambda b, j: (b * nt + j, qblk)),
                  pl.BlockSpec((ATTN_TILE, dq), lambda b, j: (b * nt + j, qblk + 1)),
                  pl.BlockSpec((ATTN_TILE, LANES), tok),
                  pl.BlockSpec((1, LANES), lambda b, j: (0, 0)),
                  pl.BlockSpec((1, dq), lambda b, j: (0, 0)),
                  pl.BlockSpec((1, dq), lambda b, j: (0, 0)),
                  pl.BlockSpec((dq, dq), lambda b, j: (0, 0))],
        out_specs=[pl.BlockSpec((ATTN_TILE, dq), tok),
                   pl.BlockSpec((ATTN_TILE, dq), tok),
                   pl.BlockSpec((ATTN_TILE, LANES), tok),
                   pl.BlockSpec((1, 1, SUBLANES, ATTN_TILE), lambda b, j: (b, j, 0, 0))],
        out_shape=[jax.ShapeDtypeStruct((t, dq), BF16), jax.ShapeDtypeStruct((t, dq), BF16),
                   jax.ShapeDtypeStruct((t, LANES), F32),
                   jax.ShapeDtypeStruct((batch, nt, SUBLANES, ATTN_TILE), F32)],
        scratch_shapes=[pltpu.VMEM((1, LANES), F32)],
        compiler_params=_params("parallel", "arbitrary"),
        name="fox_prep",
    )(proj, proj, fx, fb, qg, kg, ones)


def _fox_attn_kernel(q_ref, k_ref, v_ref, f_ref, ft_ref, y_ref, m_ref, l_ref, acc_ref, *, heads, head_dim):
    tq = q_ref.shape[0]
    i = pl.program_id(1)
    pair = LANES // head_dim
    lane = lax.broadcasted_iota(I32, (tq, LANES), 1)
    rowi = lax.broadcasted_iota(I32, (tq, tq), 0)
    coli = lax.broadcasted_iota(I32, (tq, tq), 1)

    for p in range(heads // pair):
        cols = slice(p * LANES, (p + 1) * LANES)
        q2 = q_ref[:, cols]
        qs = [jnp.where((lane >= s * head_dim) & (lane < (s + 1) * head_dim), q2, jnp.zeros_like(q2))
              for s in range(pair)]
        fq = [f_ref[:, p * pair + s:p * pair + s + 1] for s in range(pair)]
        for s in range(pair):
            m_ref[s] = jnp.full((tq, 1), NEG_BIG, F32)
            l_ref[s] = jnp.zeros((tq, 1), F32)
            acc_ref[s] = jnp.zeros((tq, LANES), F32)

        def tile(j, masked):
            start = pl.multiple_of(j * tq, tq)
            k2 = k_ref[pl.ds(start, tq), cols]
            v2 = v_ref[pl.ds(start, tq), cols]
            fk = ft_ref[0, j]
            for s in range(pair):
                h = p * pair + s
                logit = lax.dot_general(qs[s], k2, (((1,), (1,)), ((), ())), preferred_element_type=F32)
                logit = logit + (fq[s] - fk[h:h + 1, :])
                if masked:
                    logit = jnp.where(coli <= rowi, logit, NEG_BIG)
                m_old = m_ref[s]
                m_new = jnp.maximum(m_old, jnp.max(logit, axis=1, keepdims=True))
                alpha = jnp.exp(m_old - m_new)
                pr = jnp.exp(logit - m_new)
                l_ref[s] = alpha * l_ref[s] + jnp.sum(pr, axis=1, keepdims=True)
                acc_ref[s] = alpha * acc_ref[s] + jnp.dot(pr.astype(BF16), v2, preferred_element_type=F32)
                m_ref[s] = m_new

        def body(j, carry):
            tile(j, False)
            return carry

        lax.fori_loop(0, i, body, 0)
        tile(i, True)

        out = acc_ref[0] / l_ref[0]
        for s in range(1, pair):
            out = jnp.where(lane >= s * head_dim, acc_ref[s] / l_ref[s], out)
        y_ref[:, cols] = out.astype(y_ref.dtype)


def _fox_attn(qn, kn, proj, f, ft, batch, seq, heads, head_dim):
    t, dq = qn.shape
    nq = seq // ATTN_TILE
    vcol = proj.shape[1] // dq - 1
    pair = LANES // head_dim
    return pl.pallas_call(
        functools.partial(_fox_attn_kernel, heads=heads, head_dim=head_dim),
        grid=(batch, nq),
        in_specs=[pl.BlockSpec((ATTN_TILE, dq), lambda b, i: (b * nq + i, 0)),
                  pl.BlockSpec((seq, dq), lambda b, i: (b, 0)),
                  pl.BlockSpec((seq, dq), lambda b, i: (b, vcol)),
                  pl.BlockSpec((ATTN_TILE, LANES), lambda b, i: (b * nq + i, 0)),
                  pl.BlockSpec((1, nq, SUBLANES, ATTN_TILE), lambda b, i: (b, 0, 0, 0))],
        out_specs=pl.BlockSpec((ATTN_TILE, dq), lambda b, i: (b * nq + i, 0)),
        out_shape=jax.ShapeDtypeStruct((t, dq), BF16),
        scratch_shapes=[pltpu.VMEM((pair, ATTN_TILE, 1), F32), pltpu.VMEM((pair, ATTN_TILE, 1), F32),
                        pltpu.VMEM((pair, ATTN_TILE, LANES), F32)],
        compiler_params=_params("parallel", "arbitrary"),
        name="fox_attn",
    )(qn, kn, proj, f, ft)


def _gla_kernel(q_ref, k_ref, v_ref, r_ref, gx_ref, w2_ref, gb_ref, og_ref, y_ref, st_ref, *, heads, scale):
    rows = q_ref.shape[0]
    dk = q_ref.shape[1] // heads
    dv = v_ref.shape[1] // heads
    ck = GLA_CHUNK

    @pl.when(pl.program_id(1) == 0)
    def _():
        st_ref[...] = jnp.zeros_like(st_ref)

    g = jnp.dot(gx_ref[...].astype(BF16), w2_ref[...], preferred_element_type=F32) + gb_ref[...]
    log_g = -_softplus(-g) * (1.0 / GATE_TAU)
    rowc = lax.broadcasted_iota(I32, (ck, ck), 0)
    colc = lax.broadcasted_iota(I32, (ck, ck), 1)

    for c in range(rows // ck):
        rs = slice(c * ck, (c + 1) * ck)
        for h in range(heads):
            ks = slice(h * dk, (h + 1) * dk)
            vs = slice(h * dv, (h + 1) * dv)
            cum = _cumsum_rows(log_g[rs, ks])
            last = cum[ck - 1:ck, :]
            kf = k_ref[rs, ks].astype(F32)
            q_b = (q_ref[rs, ks].astype(F32) * scale * jnp.exp(cum)).astype(BF16)
            k_b = (kf * jnp.exp(-cum)).astype(BF16)
            ke_b = (kf * jnp.exp(last - cum)).astype(BF16)
            v = v_ref[rs, vs]
            intra = lax.dot_general(q_b, k_b, (((1,), (1,)), ((), ())), preferred_element_type=F32)
            intra = jnp.where(colc <= rowc, intra, 0.0)
            state = st_ref[h]
            o = (jnp.dot(intra.astype(BF16), v, preferred_element_type=F32)
                 + lax.dot_general(q_b, state.astype(BF16), (((1,), (1,)), ((), ())),
                                   preferred_element_type=F32))
            st_ref[h] = state * jnp.exp(last) + lax.dot_general(
                v, ke_b, (((0,), (0,)), ((), ())), preferred_element_type=F32)
            ms = jnp.mean(o * o, axis=-1, keepdims=True)
            o = o * lax.rsqrt(ms + EPS) * og_ref[...]
            rg = r_ref[rs, vs].astype(F32)
            y_ref[rs, vs] = (o * (rg * _sigmoid(rg))).astype(y_ref.dtype)


def _gla(proj, gx, gate_w2, gate_b, out_gain, batch, seq, heads):
    t = proj.shape[0]
    rank, d_k = gate_w2.shape
    d_v = heads * out_gain.shape[0]
    nt = seq // SEQ_TILE
    kb = d_k // d_k
    w2 = jnp.zeros((LANES, d_k), F32).at[:rank].set(gate_w2).astype(BF16)
    tok = lambda b, j: (b * nt + j, 0)
    vblk = 2 * d_k // d_v
    return pl.pallas_call(
        functools.partial(_gla_kernel, heads=heads, scale=(d_k // heads) ** -0.5),
        grid=(batch, nt),
        in_specs=[pl.BlockSpec((SEQ_TILE, d_k), lambda b, j: (b * nt + j, 0)),
                  pl.BlockSpec((SEQ_TILE, d_k), lambda b, j: (b * nt + j, kb)),
                  pl.BlockSpec((SEQ_TILE, d_v), lambda b, j: (b * nt + j, vblk)),
                  pl.BlockSpec((SEQ_TILE, d_v), lambda b, j: (b * nt + j, vblk + 1)),
                  pl.BlockSpec((SEQ_TILE, LANES), tok),
                  pl.BlockSpec((LANES, d_k), lambda b, j: (0, 0)),
                  pl.BlockSpec((1, d_k), lambda b, j: (0, 0)),
                  pl.BlockSpec((1, d_v // heads), lambda b, j: (0, 0))],
        out_specs=pl.BlockSpec((SEQ_TILE, d_v), tok),
        out_shape=jax.ShapeDtypeStruct((t, d_v), BF16),
        scratch_shapes=[pltpu.VMEM((heads, d_v // heads, d_k // heads), F32)],
        compiler_params=_params("parallel", "arbitrary"),
        name="gla",
    )(proj, proj, proj, proj, gx, w2, gate_b[None, :], out_gain[None, :])


def _route_kernel(x_ref, gain_ref, sc_ref, sh_ref, rw_ref, rb_ref, tri_ref,
                  h_ref, eidx_ref, gw_ref, rank_ref, cnt_ref, *, n_groups):
    n_exp = rw_ref.shape[0]
    epg = n_exp // n_groups
    tm = x_ref.shape[0]

    @pl.when(pl.program_id(0) == 0)
    def _():
        cnt_ref[...] = jnp.zeros_like(cnt_ref)

    h = _rms_mod(x_ref[...], gain_ref[...], sc_ref[0], sh_ref[0])
    h_ref[...] = h.astype(h_ref.dtype)
    logits = lax.dot_general(rw_ref[...], h, (((1,), (1,)), ((), ())),
                             preferred_element_type=F32, precision=HIGHEST)
    scores = _sigmoid(logits)
    sel = scores + rb_ref[...]

    sub = lax.broadcasted_iota(I32, (epg, tm), 0)

    def top2(v):
        m1 = jnp.max(v, axis=0, keepdims=True)
        i1 = jnp.min(jnp.where(v == m1, sub, epg), axis=0, keepdims=True)
        v2 = jnp.where(sub == i1, -jnp.inf, v)
        m2 = jnp.max(v2, axis=0, keepdims=True)
        i2 = jnp.min(jnp.where(v2 == m2, sub, epg), axis=0, keepdims=True)
        return m1, i1, m2, i2

    groups = [sel[g * epg:(g + 1) * epg, :] for g in range(n_groups)]
    best = None
    for g in range(n_groups):
        m1, _, m2, _ = top2(groups[g])
        gs = m1 + m2
        if g == 0:
            best, g_idx = gs, jnp.zeros((1, tm), I32)
        else:
            better = gs > best
            g_idx = jnp.where(better, g, g_idx)
            best = jnp.where(better, gs, best)
    in_group = groups[0]
    sc_group = scores[0:epg, :]
    for g in range(1, n_groups):
        in_group = jnp.where(g_idx == g, groups[g], in_group)
        sc_group = jnp.where(g_idx == g, scores[g * epg:(g + 1) * epg, :], sc_group)
    _, i1, _, i2 = top2(in_group)
    w0 = jnp.sum(jnp.where(sub == i1, sc_group, 0.0), axis=0, keepdims=True)
    w1 = jnp.sum(jnp.where(sub == i2, sc_group, 0.0), axis=0, keepdims=True)
    wsum = w0 + w1
    e0 = g_idx * epg + i1
    e1 = g_idx * epg + i2
    eidx_ref[0:1, :] = e0
    eidx_ref[1:2, :] = e1
    gw_ref[0:1, :] = w0 / wsum
    gw_ref[1:2, :] = w1 / wsum

    erow = lax.broadcasted_iota(I32, (n_exp, tm), 0)
    oh0 = erow == e0
    oh1 = erow == e1
    taken = jnp.where(oh0 | oh1, 1.0, 0.0)
    before = jnp.dot(taken.astype(BF16), tri_ref[...], preferred_element_type=F32) + cnt_ref[:, 0:1]
    r0 = jnp.sum(jnp.where(oh0, before, 0.0), axis=0, keepdims=True)
    r1 = jnp.sum(jnp.where(oh1, before, 0.0), axis=0, keepdims=True)
    rank_ref[0:1, :] = r0.astype(I32)
    rank_ref[1:2, :] = r1.astype(I32)
    cnt_ref[...] = cnt_ref[...] + jnp.sum(taken, axis=1, keepdims=True)


def _route(x2, gain, sc, sh, router_w, router_b, seq):
    t, d = x2.shape
    n_exp = router_w.shape[1]
    tps = seq // TOKEN_TILE
    pos = jnp.arange(TOKEN_TILE)
    tri = (pos[:, None] < pos[None, :]).astype(BF16)
    tokl = lambda i: (0, i)
    return pl.pallas_call(
        functools.partial(_route_kernel, n_groups=N_GROUPS),
        grid=(t // TOKEN_TILE,),
        in_specs=[pl.BlockSpec((TOKEN_TILE, d), lambda i: (i, 0)),
                  pl.BlockSpec((1, d), lambda i: (0, 0)),
                  pl.BlockSpec((1, 1, d), lambda i: (i // tps, 0, 0)),
                  pl.BlockSpec((1, 1, d), lambda i: (i // tps, 0, 0)),
                  pl.BlockSpec((n_exp, d), lambda i: (0, 0)),
                  pl.BlockSpec((n_exp, 1), lambda i: (0, 0)),
                  pl.BlockSpec((TOKEN_TILE, TOKEN_TILE), lambda i: (0, 0))],
        out_specs=[pl.BlockSpec((TOKEN_TILE, d), lambda i: (i, 0)),
                   pl.BlockSpec((TOP_K, TOKEN_TILE), tokl),
                   pl.BlockSpec((TOP_K, TOKEN_TILE), tokl),
                   pl.BlockSpec((TOP_K, TOKEN_TILE), tokl),
                   pl.BlockSpec((n_exp, LANES), lambda i: (0, 0))],
        out_shape=[jax.ShapeDtypeStruct((t, d), BF16),
                   jax.ShapeDtypeStruct((TOP_K, t), I32),
                   jax.ShapeDtypeStruct((TOP_K, t), F32),
                   jax.ShapeDtypeStruct((TOP_K, t), I32),
                   jax.ShapeDtypeStruct((n_exp, LANES), F32)],
        compiler_params=_params("arbitrary"),
        name="moe_route",
    )(x2, gain, sc, sh, router_w.T, router_b[:, None], tri)


def _expert_kernel(blk_e_ref, nact_ref, x_ref, w1_ref, w3_ref, w2_ref, o_ref, w1b, w3b, w2b):
    b = pl.program_id(0)

    @pl.when(b < nact_ref[0])
    def _():
        changed = blk_e_ref[b] != blk_e_ref[jnp.maximum(b - 1, 0)]

        @pl.when((b == 0) | changed)
        def _():
            w1b[...] = w1_ref[0, 0].astype(BF16)
            w3b[...] = w3_ref[0, 0].astype(BF16)
            w2b[...] = w2_ref[0, 0].astype(BF16)

        xb = x_ref[...]
        a = jnp.dot(xb, w1b[...], preferred_element_type=F32)
        hid = (a * _sigmoid(a)) * jnp.dot(xb, w3b[...], preferred_element_type=F32)
        o_ref[...] = jnp.dot(hid.astype(BF16), w2b[...], preferred_element_type=F32).astype(o_ref.dtype)


def _experts(x_pad, blk_e, n_active, w1, w3, w2, layer):
    p_len, d = x_pad.shape
    de = w1.shape[-1]
    n_blocks = p_len // EXPERT_ROWS

    def row_map(b, blk_e_ref, nact_ref):
        return (jnp.minimum(b, nact_ref[0] - 1), 0)

    def w_map(b, blk_e_ref, nact_ref):
        return (layer, blk_e_ref[jnp.minimum(b, nact_ref[0] - 1)], 0, 0)

    return pl.pallas_call(
        _expert_kernel,
        grid_spec=pltpu.PrefetchScalarGridSpec(
            num_scalar_prefetch=2,
            grid=(n_blocks,),
            in_specs=[pl.BlockSpec((EXPERT_ROWS, d), row_map),
                      pl.BlockSpec((1, 1, d, de), w_map),
                      pl.BlockSpec((1, 1, d, de), w_map),
                      pl.BlockSpec((1, 1, de, d), w_map)],
            out_specs=pl.BlockSpec((EXPERT_ROWS, d), row_map),
            scratch_shapes=[pltpu.VMEM((d, de), BF16), pltpu.VMEM((d, de), BF16), pltpu.VMEM((de, d), BF16)]),
        out_shape=jax.ShapeDtypeStruct((p_len, d), BF16),
        compiler_params=_params("arbitrary"),
        name="moe_experts",
    )(blk_e, n_active, x_pad, w1, w3, w2)


def _combine_kernel(x_ref, g_ref, y0_ref, y1_ref, gw_ref, o_ref):
    moe = (y0_ref[...].astype(F32) * gw_ref[:, 0:1] + y1_ref[...].astype(F32) * gw_ref[:, 1:2])
    o_ref[...] = x_ref[...] + g_ref[0] * moe


def _combine(x2, g, y0, y1, gw_t, seq):
    t, d = x2.shape
    tps = seq // TOKEN_TILE
    tok = lambda i: (i, 0)
    return pl.pallas_call(
        _combine_kernel,
        grid=(t // TOKEN_TILE,),
        in_specs=[pl.BlockSpec((TOKEN_TILE, d), tok),
                  pl.BlockSpec((1, 1, d), lambda i: (i // tps, 0, 0)),
                  pl.BlockSpec((TOKEN_TILE, d), tok),
                  pl.BlockSpec((TOKEN_TILE, d), tok),
                  pl.BlockSpec((TOKEN_TILE, TOP_K), tok)],
        out_specs=pl.BlockSpec((TOKEN_TILE, d), tok),
        out_shape=jax.ShapeDtypeStruct((t, d), F32),
        compiler_params=_params("parallel"),
        name="moe_combine",
    )(x2, g, y0, y1, gw_t)


def _moe(x2, gain, sc, sh, g, router_w, router_b, w1, w3, w2, layer, seq):
    t, d = x2.shape
    n_exp = router_w.shape[1]
    h, e_idx, gw, rank, cnt = _route(x2, gain, sc, sh, router_w, router_b, seq)
    counts = cnt[:, 0].astype(I32)
    padded = (counts + EXPERT_ROWS - 1) // EXPERT_ROWS * EXPERT_ROWS
    pend = jnp.cumsum(padded)
    pstart = pend - padded
    dest = pstart[e_idx] + rank
    n_blocks = -(-(t * TOP_K) // EXPERT_ROWS) + n_exp
    p_len = n_blocks * EXPERT_ROWS
    blk_start = jnp.arange(n_blocks, dtype=I32) * EXPERT_ROWS
    blk_e = jnp.minimum(jnp.sum((pend[None, :] <= blk_start[:, None]).astype(I32), axis=1), n_exp - 1)
    n_active = (pend[-1:] // EXPERT_ROWS).astype(I32)
    tok_ids = jnp.broadcast_to(jnp.arange(t, dtype=I32)[None, :], dest.shape)
    src = jnp.zeros((p_len,), I32).at[dest.reshape(-1)].set(tok_ids.reshape(-1))
    x_pad = jnp.take(h, src, axis=0)
    y_pad = _experts(x_pad, blk_e, n_active, w1, w3, w2, layer)
    y0 = jnp.take(y_pad, dest[0], axis=0)
    y1 = jnp.take(y_pad, dest[1], axis=0)
    return _combine(x2, g, y0, y1, gw.T, seq)


def _block_diag(w):
    n, a, b = w.shape
    eye = jnp.eye(n, dtype=w.dtype)
    return (eye[:, None, :, None] * w[:, :, None, :]).reshape(n * a, n * b)


def kernel(x, c, router_w, router_b, ada_w, ada_b, norm_mix, norm_ffn, ev_w_in, ev_conv_w, ev_conv_b, ev_r_w, ev_r_b, ev_i_w, ev_i_b, ev_lam, ev_forget_b, ev_q_gain, ev_k_gain, ev_w_out, od_w_in, od_gate_w2, od_gate_b, od_out_gain, od_w_out, moe_w1, moe_w3, moe_w2):
    batch, seq, d = x.shape
    depth = ada_w.shape[0]
    x2 = x.reshape(batch * seq, d)
    mods = _ada(c, ada_w, ada_b)

    for layer in range(depth):
        sh_m, sc_m, g_m, sh_f, sc_f, g_f = [mods[layer, :, None, k * d:(k + 1) * d] for k in range(6)]
        j = layer // 2
        if layer % 2 == 0:
            d_rnn = ev_conv_w.shape[-1]
            heads, head_dim = ev_forget_b.shape[-1], ev_q_gain.shape[-1]
            d_fox = heads * head_dim
            n_main = 2 * d_rnn + 3 * d_fox
            w_in = ev_w_in[j]
            wx = jnp.zeros((d, LANES), F32).at[:, :heads].set(w_in[:, n_main:]).astype(BF16)
            proj, fx = _in_proj(x2, norm_mix[layer][None, :], sc_m, sh_m, w_in[:, :n_main].astype(BF16), wx, seq)
            wg = jnp.concatenate([_block_diag(ev_r_w[j]), _block_diag(ev_i_w[j])], axis=1).astype(BF16)
            bg = jnp.concatenate([ev_r_b[j].reshape(-1), ev_i_b[j].reshape(-1)])[None, :]
            y_rnn = _rglru(proj, ev_conv_w[j], ev_conv_b[j][None, :], wg, bg, ev_lam[j][None, :], batch, seq)
            qn, kn, f, ft = _fox_prep(proj, fx, ev_forget_b[j], ev_q_gain[j], ev_k_gain[j],
                                      batch, seq, heads, head_dim)
            y_fox = _fox_attn(qn, kn, proj, f, ft, batch, seq, heads, head_dim)
            w_out = ev_w_out[j].astype(BF16)
            x2 = _out_proj([y_rnn, y_fox], [w_out[:d_rnn], w_out[d_rnn:]], x2, g_m, seq)
        else:
            rank, d_k = od_gate_w2.shape[1:]
            d_v = od_w_out.shape[1]
            heads = d_v // od_out_gain.shape[-1]
            n_main = 2 * d_k + 2 * d_v
            w_in = od_w_in[j]
            wx = jnp.zeros((d, LANES), F32).at[:, :rank].set(w_in[:, n_main:]).astype(BF16)
            proj, gx = _in_proj(x2, norm_mix[layer][None, :], sc_m, sh_m, w_in[:, :n_main].astype(BF16), wx, seq)
            y = _gla(proj, gx, od_gate_w2[j], od_gate_b[j], od_out_gain[j], batch, seq, heads)
            x2 = _out_proj([y], [od_w_out[j].astype(BF16)], x2, g_m, seq)
        x2 = _moe(x2, norm_ffn[layer][None, :], sc_f, sh_f, g_f, router_w, router_b,
                  moe_w1, moe_w3, moe_w2, layer, seq)
    return x2.reshape(batch, seq, d)
```

```python
import functools

import jax
import jax.numpy as jnp
from jax import lax
from jax.experimental import pallas as pl
from jax.experimental.pallas import tpu as pltpu

F32 = jnp.float32
BF16 = jnp.bfloat16
I32 = jnp.int32
HIGHEST = lax.Precision.HIGHEST

EPS = 1e-6
RGLRU_C = 8.0
CONV_W = 4
GATE_TAU = 16.0
GLA_CHUNK = 64
N_GROUPS = 4
TOP_K = 2
NEG_BIG = -1e30
LOG2E = 1.4426950408889634

LANES = 128
SUBLANES = 8
TOKEN_TILE = 512
SEQ_TILE = 256
ATTN_TILE = 256
EXPERT_ROWS = 256
VMEM_LIMIT = 48 * 1024 * 1024


def _params(*sem):
    return pltpu.CompilerParams(dimension_semantics=sem, vmem_limit_bytes=VMEM_LIMIT)


def _rms_mod(x, gain, sc, sh):
    ms = jnp.mean(x * x, axis=-1, keepdims=True)
    return (x * lax.rsqrt(ms + EPS) * gain) * (1.0 + sc) + sh


def _softplus(z):
    return jnp.maximum(z, 0.0) + jnp.log1p(jnp.exp(-jnp.abs(z)))


def _sigmoid(z):
    return 1.0 / (1.0 + jnp.exp(-z))


def _shift_rows(v, d, fill, row):
    return jnp.where(row >= d, pltpu.roll(v, d, axis=0), fill)


def _cumsum_rows(v):
    n = v.shape[0]
    row = lax.broadcasted_iota(I32, v.shape, 0)
    d = 1
    while d < n:
        v = v + _shift_rows(v, d, 0.0, row)
        d *= 2
    return v


def _ada_kernel(c_ref, w_ref, b_ref, o_ref):
    c = c_ref[...]
    o_ref[0] = jnp.dot(c * _sigmoid(c), w_ref[0], preferred_element_type=F32, precision=HIGHEST) + b_ref[0]


def _ada(c, ada_w, ada_b):
    depth, d, n = ada_w.shape
    b = c.shape[0]
    tn = n // 4
    return pl.pallas_call(
        _ada_kernel,
        grid=(depth, n // tn),
        in_specs=[pl.BlockSpec((b, d), lambda l, j: (0, 0)),
                  pl.BlockSpec((1, d, tn), lambda l, j: (l, 0, j)),
                  pl.BlockSpec((1, 1, tn), lambda l, j: (l, 0, j))],
        out_specs=pl.BlockSpec((1, b, tn), lambda l, j: (l, 0, j)),
        out_shape=jax.ShapeDtypeStruct((depth, b, n), F32),
        compiler_params=_params("parallel", "parallel"),
        name="ada_mod",
    )(c, ada_w, ada_b.reshape(depth, 1, n))


def _in_proj_kernel(x_ref, gain_ref, sc_ref, sh_ref, w_ref, wx_ref, o_ref, ox_ref, *, n_chunk):
    h = _rms_mod(x_ref[...], gain_ref[...], sc_ref[0], sh_ref[0]).astype(BF16)
    for c in range(0, o_ref.shape[1], n_chunk):
        o_ref[:, c:c + n_chunk] = jnp.dot(h, w_ref[:, c:c + n_chunk],
                                          preferred_element_type=F32).astype(o_ref.dtype)
    ox_ref[...] = jnp.dot(h, wx_ref[...], preferred_element_type=F32)


def _in_proj(x2, gain, sc, sh, w, wx, seq):
    t, d = x2.shape
    n = w.shape[1]
    tps = seq // TOKEN_TILE
    return pl.pallas_call(
        functools.partial(_in_proj_kernel, n_chunk=512),
        grid=(t // TOKEN_TILE,),
        in_specs=[pl.BlockSpec((TOKEN_TILE, d), lambda i: (i, 0)),
                  pl.BlockSpec((1, d), lambda i: (0, 0)),
                  pl.BlockSpec((1, 1, d), lambda i: (i // tps, 0, 0)),
                  pl.BlockSpec((1, 1, d), lambda i: (i // tps, 0, 0)),
                  pl.BlockSpec((d, n), lambda i: (0, 0)),
                  pl.BlockSpec((d, LANES), lambda i: (0, 0))],
        out_specs=[pl.BlockSpec((TOKEN_TILE, n), lambda i: (i, 0)),
                   pl.BlockSpec((TOKEN_TILE, LANES), lambda i: (i, 0))],
        out_shape=[jax.ShapeDtypeStruct((t, n), BF16), jax.ShapeDtypeStruct((t, LANES), F32)],
        compiler_params=_params("parallel"),
        name="in_proj",
    )(x2, gain, sc, sh, w, wx)


def _out_proj_kernel(*refs, n_parts):
    y_refs, w_refs = refs[:n_parts], refs[n_parts:2 * n_parts]
    x_ref, g_ref, o_ref = refs[2 * n_parts:]
    acc = jnp.dot(y_refs[0][...], w_refs[0][...], preferred_element_type=F32)
    for y_ref, w_ref in zip(y_refs[1:], w_refs[1:]):
        acc = acc + jnp.dot(y_ref[...], w_ref[...], preferred_element_type=F32)
    o_ref[...] = x_ref[...] + g_ref[0] * acc


def _out_proj(ys, ws, x2, g, seq):
    t, d = x2.shape
    tps = seq // TOKEN_TILE
    n_parts = len(ys)
    in_specs = [pl.BlockSpec((TOKEN_TILE, y.shape[1]), lambda i: (i, 0)) for y in ys]
    in_specs += [pl.BlockSpec(w.shape, lambda i: (0, 0)) for w in ws]
    in_specs += [pl.BlockSpec((TOKEN_TILE, d), lambda i: (i, 0)),
                 pl.BlockSpec((1, 1, d), lambda i: (i // tps, 0, 0))]
    return pl.pallas_call(
        functools.partial(_out_proj_kernel, n_parts=n_parts),
        grid=(t // TOKEN_TILE,),
        in_specs=in_specs,
        out_specs=pl.BlockSpec((TOKEN_TILE, d), lambda i: (i, 0)),
        out_shape=jax.ShapeDtypeStruct((t, d), F32),
        compiler_params=_params("parallel"),
        name="out_proj",
    )(*ys, *ws, x2, g)


def _rglru_kernel(x_ref, gate_ref, cw_ref, cb_ref, wg_ref, bg_ref, lam_ref, y_ref, ext_ref, hc_ref):
    rows, ch = x_ref.shape

    @pl.when(pl.program_id(1) == 0)
    def _():
        ext_ref[0:SUBLANES, :] = jnp.zeros((SUBLANES, ch), F32)
        hc_ref[...] = jnp.zeros_like(hc_ref)

    x = x_ref[...].astype(F32)
    ext_ref[SUBLANES:SUBLANES + rows, :] = x
    xc = cb_ref[...] + cw_ref[CONV_W - 1:CONV_W, :] * x
    for j in range(CONV_W - 1):
        off = SUBLANES - (CONV_W - 1) + j
        xc = xc + cw_ref[j:j + 1, :] * ext_ref[off:off + rows, :]
    ext_ref[0:SUBLANES, :] = x[rows - SUBLANES:, :]

    gates = jnp.dot(xc.astype(BF16), wg_ref[...], preferred_element_type=F32) + bg_ref[...]
    r = _sigmoid(gates[:, :ch])
    gi = _sigmoid(gates[:, ch:])
    log_a = (-RGLRU_C * _softplus(-lam_ref[...])) * r
    a = jnp.exp(log_a)
    u = jnp.sqrt(1.0 - a * a) * (gi * xc)

    row = lax.broadcasted_iota(I32, (rows, ch), 0)
    d = 1
    while d < rows:
        u = u + a * _shift_rows(u, d, 0.0, row)
        a = a * _shift_rows(a, d, 1.0, row)
        d *= 2
    h = u + a * hc_ref[...]
    hc_ref[...] = h[rows - 1:rows, :]
    y_ref[...] = (h * jax.nn.gelu(gate_ref[...].astype(F32))).astype(y_ref.dtype)


def _rglru(proj, conv_w, conv_b, wg, bg, lam, batch, seq):
    t = proj.shape[0]
    ch = conv_w.shape[1]
    nt = seq // SEQ_TILE
    return pl.pallas_call(
        _rglru_kernel,
        grid=(batch, nt),
        in_specs=[pl.BlockSpec((SEQ_TILE, ch), lambda b, j: (b * nt + j, 0)),
                  pl.BlockSpec((SEQ_TILE, ch), lambda b, j: (b * nt + j, 1)),
                  pl.BlockSpec((CONV_W, ch), lambda b, j: (0, 0)),
                  pl.BlockSpec((1, ch), lambda b, j: (0, 0)),
                  pl.BlockSpec((ch, 2 * ch), lambda b, j: (0, 0)),
                  pl.BlockSpec((1, 2 * ch), lambda b, j: (0, 0)),
                  pl.BlockSpec((1, ch), lambda b, j: (0, 0))],
        out_specs=pl.BlockSpec((SEQ_TILE, ch), lambda b, j: (b * nt + j, 0)),
        out_shape=jax.ShapeDtypeStruct((t, ch), BF16),
        scratch_shapes=[pltpu.VMEM((SEQ_TILE + SUBLANES, ch), F32), pltpu.VMEM((1, ch), F32)],
        compiler_params=_params("parallel", "arbitrary"),
        name="rglru",
    )(proj, proj, conv_w, conv_b, wg, bg, lam)


def _fox_prep_kernel(q_ref, k_ref, v_ref, fx_ref, fb_ref, qg_ref, kg_ref, ones_ref, qt_ref, kn_ref, vt_ref, f_ref,
                     fc_ref, *, scale):
    @pl.when(pl.program_id(1) == 0)
    def _():
        fc_ref[...] = jnp.zeros_like(fc_ref)

    def qk_norm(v_ref, g_ref, mult):
        v = v_ref[...].astype(F32)
        ms = jnp.dot((v * v).astype(BF16), ones_ref[...], preferred_element_type=F32)
        return (v * lax.rsqrt(ms + EPS) * g_ref[...]) * mult

    qt_ref[0, 0] = qk_norm(q_ref, qg_ref, scale * LOG2E).T.astype(qt_ref.dtype)
    kn_ref[...] = qk_norm(k_ref, kg_ref, 1.0).astype(kn_ref.dtype)
    vt_ref[0, 0] = v_ref[...].astype(F32).T.astype(vt_ref.dtype)

    log_f = -_softplus(-(fx_ref[...] + fb_ref[...])) * LOG2E
    f = _cumsum_rows(log_f) + fc_ref[...]
    fc_ref[...] = f[f.shape[0] - 1:, :]
    f_ref[...] = f


def _fox_prep(proj, fx, forget_b, q_gain, k_gain, batch, seq, heads, head_dim):
    t = proj.shape[0]
    dq = heads * head_dim
    nt = seq // ATTN_TILE
    qblk = (proj.shape[1] - 3 * dq) // dq
    fb = jnp.zeros((1, LANES), F32).at[0, :heads].set(forget_b)
    qg = jnp.tile(q_gain, heads)[None, :]
    kg = jnp.tile(k_gain, heads)[None, :]
    head_of = jnp.arange(dq) // head_dim
    ones = (head_of[:, None] == head_of[None, :]).astype(BF16) * (1.0 / head_dim)
    tok = lambda b, j: (b * nt + j, 0)
    return pl.pallas_call(
        functools.partial(_fox_prep_kernel, scale=head_dim ** -0.5),
        grid=(batch, nt),
        in_specs=[pl.BlockSpec((ATTN_TILE, dq), lambda b, j: (b * nt + j, qblk)),
                  pl.BlockSpec((ATTN_TILE, dq), lambda b, j: (b * nt + j, qblk + 1)),
                  pl.BlockSpec((ATTN_TILE, dq), lambda b, j: (b * nt + j, qblk + 2)),
                  pl.BlockSpec((ATTN_TILE, LANES), tok),
                  pl.BlockSpec((1, LANES), lambda b, j: (0, 0)),
                  pl.BlockSpec((1, dq), lambda b, j: (0, 0)),
                  pl.BlockSpec((1, dq), lambda b, j: (0, 0)),
                  pl.BlockSpec((dq, dq), lambda b, j: (0, 0))],
        out_specs=[pl.BlockSpec((1, 1, dq, ATTN_TILE), lambda b, j: (b, j, 0, 0)),
                   pl.BlockSpec((ATTN_TILE, dq), tok),
                   pl.BlockSpec((1, 1, dq, ATTN_TILE), lambda b, j: (b, j, 0, 0)),
                   pl.BlockSpec((ATTN_TILE, LANES), tok)],
        out_shape=[jax.ShapeDtypeStruct((batch, nt, dq, ATTN_TILE), BF16),
                   jax.ShapeDtypeStruct((t, dq), BF16),
                   jax.ShapeDtypeStruct((batch, nt, dq, ATTN_TILE), BF16),
                   jax.ShapeDtypeStruct((t, LANES), F32)],
        scratch_shapes=[pltpu.VMEM((1, LANES), F32)],
        compiler_params=_params("parallel", "arbitrary"),
        name="fox_prep",
    )(proj, proj, proj, fx, fb, qg, kg, ones)


def _fox_attn_kernel(qt_ref, k_ref, vt_ref, f_ref, y_ref, qm_ref, lg_ref, m_ref, l_ref, acc_ref, *,
                     heads, head_dim):
    tq = y_ref.shape[0]
    i = pl.program_id(1)
    pair = LANES // head_dim
    feat = lax.broadcasted_iota(I32, (LANES, tq), 0)
    key_i = lax.broadcasted_iota(I32, (tq, tq), 0)
    qry_i = lax.broadcasted_iota(I32, (tq, tq), 1)

    for h in range(heads):
        p, s = divmod(h, pair)
        q2t = qt_ref[0, 0, p * LANES:(p + 1) * LANES, :]
        qm_ref[h] = jnp.where((feat >= s * head_dim) & (feat < (s + 1) * head_dim), q2t, jnp.zeros_like(q2t))
    m_ref[...] = jnp.full(m_ref.shape, NEG_BIG, F32)
    l_ref[...] = jnp.zeros(l_ref.shape, F32)
    acc_ref[...] = jnp.zeros(acc_ref.shape, F32)

    def tile(j, masked):
        start = pl.multiple_of(j * tq, tq)
        fk = f_ref[pl.ds(start, tq), :]
        for h in range(heads):
            p = h // pair
            k2 = k_ref[pl.ds(start, tq), p * LANES:(p + 1) * LANES]
            lg_ref[h] = jnp.dot(k2, qm_ref[h], preferred_element_type=F32)
        for h in range(heads):
            vt = vt_ref[0, j, h * head_dim:(h + 1) * head_dim, :]
            logit = lg_ref[h] - fk[:, h:h + 1]
            if masked:
                logit = jnp.where(key_i <= qry_i, logit, NEG_BIG)
            m_old = m_ref[h][0:1, :]
            m_new = jnp.maximum(m_old, jnp.max(logit, axis=0, keepdims=True))
            alpha = jnp.exp2(m_old - m_new)
            pr = jnp.exp2(logit - m_new)
            l_new = alpha * l_ref[h][0:1, :] + jnp.sum(pr, axis=0, keepdims=True)
            acc_ref[h] = alpha * acc_ref[h] + jnp.dot(vt, pr.astype(BF16), preferred_element_type=F32)
            m_ref[h] = jnp.broadcast_to(m_new, (SUBLANES, tq))
            l_ref[h] = jnp.broadcast_to(l_new, (SUBLANES, tq))

    def body(j, carry):
        tile(j, False)
        return carry

    lax.fori_loop(0, i, body, 0)
    tile(i, True)

    for p in range(heads // pair):
        out_t = jnp.concatenate([acc_ref[h] / l_ref[h][0:1, :] for h in range(p * pair, (p + 1) * pair)], axis=0)
        y_ref[:, p * LANES:(p + 1) * LANES] = out_t.T.astype(y_ref.dtype)


def _fox_attn(qt, kn, vt, f, batch, seq, heads, head_dim):
    t, dq = kn.shape
    nq = seq // ATTN_TILE
    return pl.pallas_call(
        functools.partial(_fox_attn_kernel, heads=heads, head_dim=head_dim),
        grid=(batch, nq),
        in_specs=[pl.BlockSpec((1, 1, dq, ATTN_TILE), lambda b, i: (b, i, 0, 0)),
                  pl.BlockSpec((seq, dq), lambda b, i: (b, 0)),
                  pl.BlockSpec((1, nq, dq, ATTN_TILE), lambda b, i: (b, 0, 0, 0)),
                  pl.BlockSpec((seq, LANES), lambda b, i: (b, 0))],
        out_specs=pl.BlockSpec((ATTN_TILE, dq), lambda b, i: (b * nq + i, 0)),
        out_shape=jax.ShapeDtypeStruct((t, dq), BF16),
        scratch_shapes=[pltpu.VMEM((heads, LANES, ATTN_TILE), BF16),
                        pltpu.VMEM((heads, ATTN_TILE, ATTN_TILE), F32),
                        pltpu.VMEM((heads, SUBLANES, ATTN_TILE), F32),
                        pltpu.VMEM((heads, SUBLANES, ATTN_TILE), F32),
                        pltpu.VMEM((heads, head_dim, ATTN_TILE), F32)],
        compiler_params=_params("parallel", "arbitrary"),
        name="fox_attn",
    )(qt, kn, vt, f)


def _gla_kernel(q_ref, k_ref, v_ref, r_ref, gx_ref, w2_ref, gb_ref, og_ref, y_ref, st_ref, *, heads, scale):
    rows = q_ref.shape[0]
    dk = q_ref.shape[1] // heads
    dv = v_ref.shape[1] // heads
    ck = GLA_CHUNK

    @pl.when(pl.program_id(1) == 0)
    def _():
        st_ref[...] = jnp.zeros_like(st_ref)

    g = jnp.dot(gx_ref[...].astype(BF16), w2_ref[...], preferred_element_type=F32) + gb_ref[...]
    log_g = -_softplus(-g) * (1.0 / GATE_TAU)
    rowc = lax.broadcasted_iota(I32, (ck, ck), 0)
    colc = lax.broadcasted_iota(I32, (ck, ck), 1)

    for c in range(rows // ck):
        rs = slice(c * ck, (c + 1) * ck)
        for h in range(heads):
            ks = slice(h * dk, (h + 1) * dk)
            vs = slice(h * dv, (h + 1) * dv)
            cum = _cumsum_rows(log_g[rs, ks])
            last = cum[ck - 1:ck, :]
            kf = k_ref[rs, ks].astype(F32)
            q_b = (q_ref[rs, ks].astype(F32) * scale * jnp.exp(cum)).astype(BF16)
            k_b = (kf * jnp.exp(-cum)).astype(BF16)
            ke_b = (kf * jnp.exp(last - cum)).astype(BF16)
            v = v_ref[rs, vs]
            intra = lax.dot_general(q_b, k_b, (((1,), (1,)), ((), ())), preferred_element_type=F32)
            intra = jnp.where(colc <= rowc, intra, 0.0)
            state = st_ref[h]
            o = (jnp.dot(intra.astype(BF16), v, preferred_element_type=F32)
                 + lax.dot_general(q_b, state.astype(BF16), (((1,), (1,)), ((), ())),
                                   preferred_element_type=F32))
            st_ref[h] = state * jnp.exp(last) + lax.dot_general(
                v, ke_b, (((0,), (0,)), ((), ())), preferred_element_type=F32)
            ms = jnp.mean(o * o, axis=-1, keepdims=True)
            o = o * lax.rsqrt(ms + EPS) * og_ref[...]
            rg = r_ref[rs, vs].astype(F32)
            y_ref[rs, vs] = (o * (rg * _sigmoid(rg))).astype(y_ref.dtype)


def _gla(proj, gx, gate_w2, gate_b, out_gain, batch, seq, heads):
    t = proj.shape[0]
    rank, d_k = gate_w2.shape
    d_v = heads * out_gain.shape[0]
    nt = seq // SEQ_TILE
    kb = d_k // d_k
    w2 = jnp.zeros((LANES, d_k), F32).at[:rank].set(gate_w2).astype(BF16)
    tok = lambda b, j: (b * nt + j, 0)
    vblk = 2 * d_k // d_v
    return pl.pallas_call(
        functools.partial(_gla_kernel, heads=heads, scale=(d_k // heads) ** -0.5),
        grid=(batch, nt),
        in_specs=[pl.BlockSpec((SEQ_TILE, d_k), lambda b, j: (b * nt + j, 0)),
                  pl.BlockSpec((SEQ_TILE, d_k), lambda b, j: (b * nt + j, kb)),
                  pl.BlockSpec((SEQ_TILE, d_v), lambda b, j: (b * nt + j, vblk)),
                  pl.BlockSpec((SEQ_TILE, d_v), lambda b, j: (b * nt + j, vblk + 1)),
                  pl.BlockSpec((SEQ_TILE, LANES), tok),
                  pl.BlockSpec((LANES, d_k), lambda b, j: (0, 0)),
                  pl.BlockSpec((1, d_k), lambda b, j: (0, 0)),
                  pl.BlockSpec((1, d_v // heads), lambda b, j: (0, 0))],
        out_specs=pl.BlockSpec((SEQ_TILE, d_v), tok),
        out_shape=jax.ShapeDtypeStruct((t, d_v), BF16),
        scratch_shapes=[pltpu.VMEM((heads, d_v // heads, d_k // heads), F32)],
        compiler_params=_params("parallel", "arbitrary"),
        name="gla",
    )(proj, proj, proj, proj, gx, w2, gate_b[None, :], out_gain[None, :])


def _route_kernel(x_ref, gain_ref, sc_ref, sh_ref, rw_ref, rb_ref, tri_ref,
                  h_ref, eidx_ref, gw_ref, rank_ref, cnt_ref, *, n_groups):
    n_exp = rw_ref.shape[0]
    epg = n_exp // n_groups
    tm = x_ref.shape[0]

    @pl.when(pl.program_id(0) == 0)
    def _():
        cnt_ref[...] = jnp.zeros_like(cnt_ref)

    h = _rms_mod(x_ref[...], gain_ref[...], sc_ref[0], sh_ref[0])
    h_ref[...] = h.astype(h_ref.dtype)
    logits = lax.dot_general(rw_ref[...], h, (((1,), (1,)), ((), ())),
                             preferred_element_type=F32, precision=HIGHEST)
    scores = _sigmoid(logits)
    sel = scores + rb_ref[...]

    sub = lax.broadcasted_iota(I32, (epg, tm), 0)

    def top2(v):
        m1 = jnp.max(v, axis=0, keepdims=True)
        i1 = jnp.min(jnp.where(v == m1, sub, epg), axis=0, keepdims=True)
        v2 = jnp.where(sub == i1, -jnp.inf, v)
        m2 = jnp.max(v2, axis=0, keepdims=True)
        i2 = jnp.min(jnp.where(v2 == m2, sub, epg), axis=0, keepdims=True)
        return m1, i1, m2, i2

    groups = [sel[g * epg:(g + 1) * epg, :] for g in range(n_groups)]
    best = None
    for g in range(n_groups):
        m1, _, m2, _ = top2(groups[g])
        gs = m1 + m2
        if g == 0:
            best, g_idx = gs, jnp.zeros((1, tm), I32)
        else:
            better = gs > best
            g_idx = jnp.where(better, g, g_idx)
            best = jnp.where(better, gs, best)
    in_group = groups[0]
    sc_group = scores[0:epg, :]
    for g in range(1, n_groups):
        in_group = jnp.where(g_idx == g, groups[g], in_group)
        sc_group = jnp.where(g_idx == g, scores[g * epg:(g + 1) * epg, :], sc_group)
    _, i1, _, i2 = top2(in_group)
    w0 = jnp.sum(jnp.where(sub == i1, sc_group, 0.0), axis=0, keepdims=True)
    w1 = jnp.sum(jnp.where(sub == i2, sc_group, 0.0), axis=0, keepdims=True)
    wsum = w0 + w1
    e0 = g_idx * epg + i1
    e1 = g_idx * epg + i2
    eidx_ref[0:1, :] = e0
    eidx_ref[1:2, :] = e1
    gw_ref[0:1, :] = w0 / wsum
    gw_ref[1:2, :] = w1 / wsum

    erow = lax.broadcasted_iota(I32, (n_exp, tm), 0)
    oh0 = erow == e0
    oh1 = erow == e1
    taken = jnp.where(oh0 | oh1, 1.0, 0.0)
    before = jnp.dot(taken.astype(BF16), tri_ref[...], preferred_element_type=F32) + cnt_ref[:, 0:1]
    r0 = jnp.sum(jnp.where(oh0, before, 0.0), axis=0, keepdims=True)
    r1 = jnp.sum(jnp.where(oh1, before, 0.0), axis=0, keepdims=True)
    rank_ref[0:1, :] = r0.astype(I32)
    rank_ref[1:2, :] = r1.astype(I32)
    cnt_ref[...] = cnt_ref[...] + jnp.sum(taken, axis=1, keepdims=True)


def _route(x2, gain, sc, sh, router_w, router_b, seq):
    t, d = x2.shape
    n_exp = router_w.shape[1]
    tps = seq // TOKEN_TILE
    pos = jnp.arange(TOKEN_TILE)
    tri = (pos[:, None] < pos[None, :]).astype(BF16)
    tokl = lambda i: (0, i)
    return pl.pallas_call(
        functools.partial(_route_kernel, n_groups=N_GROUPS),
        grid=(t // TOKEN_TILE,),
        in_specs=[pl.BlockSpec((TOKEN_TILE, d), lambda i: (i, 0)),
                  pl.BlockSpec((1, d), lambda i: (0, 0)),
                  pl.BlockSpec((1, 1, d), lambda i: (i // tps, 0, 0)),
                  pl.BlockSpec((1, 1, d), lambda i: (i // tps, 0, 0)),
                  pl.BlockSpec((n_exp, d), lambda i: (0, 0)),
                  pl.BlockSpec((n_exp, 1), lambda i: (0, 0)),
                  pl.BlockSpec((TOKEN_TILE, TOKEN_TILE), lambda i: (0, 0))],
        out_specs=[pl.BlockSpec((TOKEN_TILE, d), lambda i: (i, 0)),
                   pl.BlockSpec((TOP_K, TOKEN_TILE), tokl),
                   pl.BlockSpec((TOP_K, TOKEN_TILE), tokl),
                   pl.BlockSpec((TOP_K, TOKEN_TILE), tokl),
                   pl.BlockSpec((n_exp, LANES), lambda i: (0, 0))],
        out_shape=[jax.ShapeDtypeStruct((t, d), BF16),
                   jax.ShapeDtypeStruct((TOP_K, t), I32),
                   jax.ShapeDtypeStruct((TOP_K, t), F32),
                   jax.ShapeDtypeStruct((TOP_K, t), I32),
                   jax.ShapeDtypeStruct((n_exp, LANES), F32)],
        compiler_params=_params("arbitrary"),
        name="moe_route",
    )(x2, gain, sc, sh, router_w.T, router_b[:, None], tri)


def _expert_kernel(blk_e_ref, nact_ref, x_ref, w1_ref, w3_ref, w2_ref, o_ref, w1b, w3b, w2b):
    b = pl.program_id(0)

    @pl.when(b < nact_ref[0])
    def _():
        changed = blk_e_ref[b] != blk_e_ref[jnp.maximum(b - 1, 0)]

        @pl.when((b == 0) | changed)
        def _():
            w1b[...] = w1_ref[0, 0].astype(BF16)
            w3b[...] = w3_ref[0, 0].astype(BF16)
            w2b[...] = w2_ref[0, 0].astype(BF16)

        xb = x_ref[...]
        a = jnp.dot(xb, w1b[...], preferred_element_type=F32)
        hid = (a * _sigmoid(a)) * jnp.dot(xb, w3b[...], preferred_element_type=F32)
        o_ref[...] = jnp.dot(hid.astype(BF16), w2b[...], preferred_element_type=F32).astype(o_ref.dtype)


def _experts(x_pad, blk_e, n_active, w1, w3, w2, layer):
    p_len, d = x_pad.shape
    de = w1.shape[-1]
    n_blocks = p_len // EXPERT_ROWS

    def last_active(b, nact_ref):
        return jnp.maximum(jnp.minimum(b, nact_ref[0] - 1), 0)

    def row_map(b, blk_e_ref, nact_ref):
        return (last_active(b, nact_ref), 0)

    def w_map(b, blk_e_ref, nact_ref):
        return (layer, blk_e_ref[last_active(b, nact_ref)], 0, 0)

    return pl.pallas_call(
        _expert_kernel,
        grid_spec=pltpu.PrefetchScalarGridSpec(
            num_scalar_prefetch=2,
            grid=(n_blocks,),
            in_specs=[pl.BlockSpec((EXPERT_ROWS, d), row_map),
                      pl.BlockSpec((1, 1, d, de), w_map),
                      pl.BlockSpec((1, 1, d, de), w_map),
                      pl.BlockSpec((1, 1, de, d), w_map)],
            out_specs=pl.BlockSpec((EXPERT_ROWS, d), row_map),
            scratch_shapes=[pltpu.VMEM((d, de), BF16), pltpu.VMEM((d, de), BF16), pltpu.VMEM((de, d), BF16)]),
        out_shape=jax.ShapeDtypeStruct((p_len, d), BF16),
        compiler_params=_params("arbitrary"),
        name="moe_experts",
    )(blk_e, n_active, x_pad, w1, w3, w2)


def _combine_kernel(x_ref, g_ref, y0_ref, y1_ref, gw_ref, o_ref):
    moe = (y0_ref[...].astype(F32) * gw_ref[:, 0:1] + y1_ref[...].astype(F32) * gw_ref[:, 1:2])
    o_ref[...] = x_ref[...] + g_ref[0] * moe


def _combine(x2, g, y0, y1, gw_t, seq):
    t, d = x2.shape
    tps = seq // TOKEN_TILE
    tok = lambda i: (i, 0)
    return pl.pallas_call(
        _combine_kernel,
        grid=(t // TOKEN_TILE,),
        in_specs=[pl.BlockSpec((TOKEN_TILE, d), tok),
                  pl.BlockSpec((1, 1, d), lambda i: (i // tps, 0, 0)),
                  pl.BlockSpec((TOKEN_TILE, d), tok),
                  pl.BlockSpec((TOKEN_TILE, d), tok),
                  pl.BlockSpec((TOKEN_TILE, TOP_K), tok)],
        out_specs=pl.BlockSpec((TOKEN_TILE, d), tok),
        out_shape=jax.ShapeDtypeStruct((t, d), F32),
        compiler_params=_params("parallel"),
        name="moe_combine",
    )(x2, g, y0, y1, gw_t)


def _moe(x2, gain, sc, sh, g, router_w, router_b, w1, w3, w2, layer, seq):
    t, d = x2.shape
    n_exp = router_w.shape[1]
    h, e_idx, gw, rank, cnt = _route(x2, gain, sc, sh, router_w, router_b, seq)
    counts = cnt[:, 0].astype(I32)
    padded = (counts + EXPERT_ROWS - 1) // EXPERT_ROWS * EXPERT_ROWS
    pend = jnp.cumsum(padded)
    pstart = pend - padded
    dest = pstart[e_idx] + rank
    n_blocks = -(-(t * TOP_K) // EXPERT_ROWS) + n_exp
    p_len = n_blocks * EXPERT_ROWS
    blk_start = jnp.arange(n_blocks, dtype=I32) * EXPERT_ROWS
    blk_e = jnp.minimum(jnp.sum((pend[None, :] <= blk_start[:, None]).astype(I32), axis=1), n_exp - 1)
    n_active = (pend[-1:] // EXPERT_ROWS).astype(I32)
    tok_ids = jnp.broadcast_to(jnp.arange(t, dtype=I32)[None, :], dest.shape)
    src = jnp.zeros((p_len,), I32).at[dest.reshape(-1)].set(tok_ids.reshape(-1))
    x_pad = jnp.take(h, src, axis=0)
    y_pad = _experts(x_pad, blk_e, n_active, w1, w3, w2, layer)
    y0 = jnp.take(y_pad, dest[0], axis=0)
    y1 = jnp.take(y_pad, dest[1], axis=0)
    return _combine(x2, g, y0, y1, gw.T, seq)


def _block_diag(w):
    n, a, b = w.shape
    eye = jnp.eye(n, dtype=w.dtype)
    return (eye[:, None, :, None] * w[:, :, None, :]).reshape(n * a, n * b)


def kernel(x, c, router_w, router_b, ada_w, ada_b, norm_mix, norm_ffn, ev_w_in, ev_conv_w, ev_conv_b, ev_r_w, ev_r_b, ev_i_w, ev_i_b, ev_lam, ev_forget_b, ev_q_gain, ev_k_gain, ev_w_out, od_w_in, od_gate_w2, od_gate_b, od_out_gain, od_w_out, moe_w1, moe_w3, moe_w2):
    batch, seq, d = x.shape
    depth = ada_w.shape[0]
    x2 = x.reshape(batch * seq, d)
    mods = _ada(c, ada_w, ada_b)

    for layer in range(depth):
        sh_m, sc_m, g_m, sh_f, sc_f, g_f = [mods[layer, :, None, k * d:(k + 1) * d] for k in range(6)]
        j = layer // 2
        if layer % 2 == 0:
            d_rnn = ev_conv_w.shape[-1]
            heads, head_dim = ev_forget_b.shape[-1], ev_q_gain.shape[-1]
            d_fox = heads * head_dim
            n_main = 2 * d_rnn + 3 * d_fox
            w_in = ev_w_in[j]
            wx = jnp.zeros((d, LANES), F32).at[:, :heads].set(w_in[:, n_main:]).astype(BF16)
            proj, fx = _in_proj(x2, norm_mix[layer][None, :], sc_m, sh_m, w_in[:, :n_main].astype(BF16), wx, seq)
            wg = jnp.concatenate([_block_diag(ev_r_w[j]), _block_diag(ev_i_w[j])], axis=1).astype(BF16)
            bg = jnp.concatenate([ev_r_b[j].reshape(-1), ev_i_b[j].reshape(-1)])[None, :]
            y_rnn = _rglru(proj, ev_conv_w[j], ev_conv_b[j][None, :], wg, bg, ev_lam[j][None, :], batch, seq)
            qt, kn, vt, f = _fox_prep(proj, fx, ev_forget_b[j], ev_q_gain[j], ev_k_gain[j],
                                      batch, seq, heads, head_dim)
            y_fox = _fox_attn(qt, kn, vt, f, batch, seq, heads, head_dim)
            w_out = ev_w_out[j].astype(BF16)
            x2 = _out_proj([y_rnn, y_fox], [w_out[:d_rnn], w_out[d_rnn:]], x2, g_m, seq)
        else:
            rank, d_k = od_gate_w2.shape[1:]
            d_v = od_w_out.shape[1]
            heads = d_v // od_out_gain.shape[-1]
            n_main = 2 * d_k + 2 * d_v
            w_in = od_w_in[j]
            wx = jnp.zeros((d, LANES), F32).at[:, :rank].set(w_in[:, n_main:]).astype(BF16)
            proj, gx = _in_proj(x2, norm_mix[layer][None, :], sc_m, sh_m, w_in[:, :n_main].astype(BF16), wx, seq)
            y = _gla(proj, gx, od_gate_w2[j], od_gate_b[j], od_out_gain[j], batch, seq, heads)
            x2 = _out_proj([y], [od_w_out[j].astype(BF16)], x2, g_m, seq)
        x2 = _moe(x2, norm_ffn[layer][None, :], sc_f, sh_f, g_f, router_w, router_b,
                  moe_w1, moe_w3, moe_w2, layer, seq)
    return x2.reshape(batch, seq, d)
```

```python
import functools

import jax
import jax.numpy as jnp
from jax import lax
from jax.experimental import pallas as pl
from jax.experimental.pallas import tpu as pltpu

F32 = jnp.float32
BF16 = jnp.bfloat16
I32 = jnp.int32
HIGHEST = lax.Precision.HIGHEST

EPS = 1e-6
RGLRU_C = 8.0
CONV_W = 4
GATE_TAU = 16.0
GLA_CHUNK = 64
N_GROUPS = 4
TOP_K = 2
NEG_BIG = -1e30
LOG2E = 1.4426950408889634

LANES = 128
SUBLANES = 8
TOKEN_TILE = 512
SEQ_TILE = 256
ATTN_TILE = 256
EXPERT_ROWS = 256
DMA_UNROLL = 16
VMEM_LIMIT = 48 * 1024 * 1024


def _params(*sem):
    return pltpu.CompilerParams(dimension_semantics=sem, vmem_limit_bytes=VMEM_LIMIT)


def _rms_mod(x, gain, sc, sh):
    ms = jnp.mean(x * x, axis=-1, keepdims=True)
    return (x * lax.rsqrt(ms + EPS) * gain) * (1.0 + sc) + sh


def _softplus(z):
    return jnp.maximum(z, 0.0) + jnp.log1p(jnp.exp(-jnp.abs(z)))


def _sigmoid(z):
    return 1.0 / (1.0 + jnp.exp(-z))


def _shift_rows(v, d, fill, row):
    return jnp.where(row >= d, pltpu.roll(v, d, axis=0), fill)


def _cumsum_rows(v):
    n = v.shape[0]
    row = lax.broadcasted_iota(I32, v.shape, 0)
    d = 1
    while d < n:
        v = v + _shift_rows(v, d, 0.0, row)
        d *= 2
    return v


def _ada_kernel(c_ref, w_ref, b_ref, o_ref):
    c = c_ref[...]
    o_ref[0] = jnp.dot(c * _sigmoid(c), w_ref[0], preferred_element_type=F32, precision=HIGHEST) + b_ref[0]


def _ada(c, ada_w, ada_b):
    depth, d, n = ada_w.shape
    b = c.shape[0]
    tn = n // 4
    return pl.pallas_call(
        _ada_kernel,
        grid=(depth, n // tn),
        in_specs=[pl.BlockSpec((b, d), lambda l, j: (0, 0)),
                  pl.BlockSpec((1, d, tn), lambda l, j: (l, 0, j)),
                  pl.BlockSpec((1, 1, tn), lambda l, j: (l, 0, j))],
        out_specs=pl.BlockSpec((1, b, tn), lambda l, j: (l, 0, j)),
        out_shape=jax.ShapeDtypeStruct((depth, b, n), F32),
        compiler_params=_params("parallel", "parallel"),
        name="ada_mod",
    )(c, ada_w, ada_b.reshape(depth, 1, n))


def _in_proj_kernel(x_ref, gain_ref, sc_ref, sh_ref, w_ref, wx_ref, o_ref, ox_ref, *, n_chunk):
    h = _rms_mod(x_ref[...], gain_ref[...], sc_ref[0], sh_ref[0]).astype(BF16)
    for c in range(0, o_ref.shape[1], n_chunk):
        o_ref[:, c:c + n_chunk] = jnp.dot(h, w_ref[:, c:c + n_chunk],
                                          preferred_element_type=F32).astype(o_ref.dtype)
    ox_ref[...] = jnp.dot(h, wx_ref[...], preferred_element_type=F32)


def _in_proj(x2, gain, sc, sh, w, wx, seq):
    t, d = x2.shape
    n = w.shape[1]
    tps = seq // TOKEN_TILE
    return pl.pallas_call(
        functools.partial(_in_proj_kernel, n_chunk=512),
        grid=(t // TOKEN_TILE,),
        in_specs=[pl.BlockSpec((TOKEN_TILE, d), lambda i: (i, 0)),
                  pl.BlockSpec((1, d), lambda i: (0, 0)),
                  pl.BlockSpec((1, 1, d), lambda i: (i // tps, 0, 0)),
                  pl.BlockSpec((1, 1, d), lambda i: (i // tps, 0, 0)),
                  pl.BlockSpec((d, n), lambda i: (0, 0)),
                  pl.BlockSpec((d, LANES), lambda i: (0, 0))],
        out_specs=[pl.BlockSpec((TOKEN_TILE, n), lambda i: (i, 0)),
                   pl.BlockSpec((TOKEN_TILE, LANES), lambda i: (i, 0))],
        out_shape=[jax.ShapeDtypeStruct((t, n), BF16), jax.ShapeDtypeStruct((t, LANES), F32)],
        compiler_params=_params("parallel"),
        name="in_proj",
    )(x2, gain, sc, sh, w, wx)


def _out_proj_kernel(*refs, n_parts):
    y_refs, w_refs = refs[:n_parts], refs[n_parts:2 * n_parts]
    x_ref, g_ref, o_ref = refs[2 * n_parts:]
    acc = jnp.dot(y_refs[0][...], w_refs[0][...], preferred_element_type=F32)
    for y_ref, w_ref in zip(y_refs[1:], w_refs[1:]):
        acc = acc + jnp.dot(y_ref[...], w_ref[...], preferred_element_type=F32)
    o_ref[...] = x_ref[...] + g_ref[0] * acc


def _out_proj(ys, ws, x2, g, seq):
    t, d = x2.shape
    tps = seq // TOKEN_TILE
    n_parts = len(ys)
    in_specs = [pl.BlockSpec((TOKEN_TILE, y.shape[1]), lambda i: (i, 0)) for y in ys]
    in_specs += [pl.BlockSpec(w.shape, lambda i: (0, 0)) for w in ws]
    in_specs += [pl.BlockSpec((TOKEN_TILE, d), lambda i: (i, 0)),
                 pl.BlockSpec((1, 1, d), lambda i: (i // tps, 0, 0))]
    return pl.pallas_call(
        functools.partial(_out_proj_kernel, n_parts=n_parts),
        grid=(t // TOKEN_TILE,),
        in_specs=in_specs,
        out_specs=pl.BlockSpec((TOKEN_TILE, d), lambda i: (i, 0)),
        out_shape=jax.ShapeDtypeStruct((t, d), F32),
        compiler_params=_params("parallel"),
        name="out_proj",
    )(*ys, *ws, x2, g)


def _rglru_kernel(x_ref, gate_ref, cw_ref, cb_ref, wg_ref, bg_ref, lam_ref, y_ref, ext_ref, hc_ref):
    rows, ch = x_ref.shape

    @pl.when(pl.program_id(1) == 0)
    def _():
        ext_ref[0:SUBLANES, :] = jnp.zeros((SUBLANES, ch), F32)
        hc_ref[...] = jnp.zeros_like(hc_ref)

    x = x_ref[...].astype(F32)
    ext_ref[SUBLANES:SUBLANES + rows, :] = x
    xc = cb_ref[...] + cw_ref[CONV_W - 1:CONV_W, :] * x
    for j in range(CONV_W - 1):
        off = SUBLANES - (CONV_W - 1) + j
        xc = xc + cw_ref[j:j + 1, :] * ext_ref[off:off + rows, :]
    ext_ref[0:SUBLANES, :] = x[rows - SUBLANES:, :]

    gates = jnp.dot(xc.astype(BF16), wg_ref[...], preferred_element_type=F32) + bg_ref[...]
    r = _sigmoid(gates[:, :ch])
    gi = _sigmoid(gates[:, ch:])
    log_a = (-RGLRU_C * _softplus(-lam_ref[...])) * r
    a = jnp.exp(log_a)
    u = jnp.sqrt(1.0 - a * a) * (gi * xc)

    row = lax.broadcasted_iota(I32, (rows, ch), 0)
    d = 1
    while d < rows:
        u = u + a * _shift_rows(u, d, 0.0, row)
        a = a * _shift_rows(a, d, 1.0, row)
        d *= 2
    h = u + a * hc_ref[...]
    hc_ref[...] = h[rows - 1:rows, :]
    y_ref[...] = (h * jax.nn.gelu(gate_ref[...].astype(F32))).astype(y_ref.dtype)


def _rglru(proj, conv_w, conv_b, wg, bg, lam, batch, seq):
    t = proj.shape[0]
    ch = conv_w.shape[1]
    nt = seq // SEQ_TILE
    return pl.pallas_call(
        _rglru_kernel,
        grid=(batch, nt),
        in_specs=[pl.BlockSpec((SEQ_TILE, ch), lambda b, j: (b * nt + j, 0)),
                  pl.BlockSpec((SEQ_TILE, ch), lambda b, j: (b * nt + j, 1)),
                  pl.BlockSpec((CONV_W, ch), lambda b, j: (0, 0)),
                  pl.BlockSpec((1, ch), lambda b, j: (0, 0)),
                  pl.BlockSpec((ch, 2 * ch), lambda b, j: (0, 0)),
                  pl.BlockSpec((1, 2 * ch), lambda b, j: (0, 0)),
                  pl.BlockSpec((1, ch), lambda b, j: (0, 0))],
        out_specs=pl.BlockSpec((SEQ_TILE, ch), lambda b, j: (b * nt + j, 0)),
        out_shape=jax.ShapeDtypeStruct((t, ch), BF16),
        scratch_shapes=[pltpu.VMEM((SEQ_TILE + SUBLANES, ch), F32), pltpu.VMEM((1, ch), F32)],
        compiler_params=_params("parallel", "arbitrary"),
        name="rglru",
    )(proj, proj, conv_w, conv_b, wg, bg, lam)


def _fox_prep_kernel(q_ref, k_ref, v_ref, fx_ref, fb_ref, qg_ref, kg_ref, ones_ref, qt_ref, kn_ref, vt_ref, f_ref,
                     fc_ref, *, scale):
    @pl.when(pl.program_id(1) == 0)
    def _():
        fc_ref[...] = jnp.zeros_like(fc_ref)

    def qk_norm(v_ref, g_ref, mult):
        v = v_ref[...].astype(F32)
        ms = jnp.dot((v * v).astype(BF16), ones_ref[...], preferred_element_type=F32)
        return (v * lax.rsqrt(ms + EPS) * g_ref[...]) * mult

    qt_ref[0, 0] = qk_norm(q_ref, qg_ref, scale * LOG2E).T.astype(qt_ref.dtype)
    kn_ref[...] = qk_norm(k_ref, kg_ref, 1.0).astype(kn_ref.dtype)
    vt_ref[0, 0] = v_ref[...].astype(F32).T.astype(vt_ref.dtype)

    log_f = -_softplus(-(fx_ref[...] + fb_ref[...])) * LOG2E
    f = _cumsum_rows(log_f) + fc_ref[...]
    fc_ref[...] = f[f.shape[0] - 1:, :]
    f_ref[...] = f


def _fox_prep(proj, fx, forget_b, q_gain, k_gain, batch, seq, heads, head_dim):
    t = proj.shape[0]
    dq = heads * head_dim
    nt = seq // ATTN_TILE
    qblk = (proj.shape[1] - 3 * dq) // dq
    fb = jnp.zeros((1, LANES), F32).at[0, :heads].set(forget_b)
    qg = jnp.tile(q_gain, heads)[None, :]
    kg = jnp.tile(k_gain, heads)[None, :]
    head_of = jnp.arange(dq) // head_dim
    ones = (head_of[:, None] == head_of[None, :]).astype(BF16) * (1.0 / head_dim)
    tok = lambda b, j: (b * nt + j, 0)
    return pl.pallas_call(
        functools.partial(_fox_prep_kernel, scale=head_dim ** -0.5),
        grid=(batch, nt),
        in_specs=[pl.BlockSpec((ATTN_TILE, dq), lambda b, j: (b * nt + j, qblk)),
                  pl.BlockSpec((ATTN_TILE, dq), lambda b, j: (b * nt + j, qblk + 1)),
                  pl.BlockSpec((ATTN_TILE, dq), lambda b, j: (b * nt + j, qblk + 2)),
                  pl.BlockSpec((ATTN_TILE, LANES), tok),
                  pl.BlockSpec((1, LANES), lambda b, j: (0, 0)),
                  pl.BlockSpec((1, dq), lambda b, j: (0, 0)),
                  pl.BlockSpec((1, dq), lambda b, j: (0, 0)),
                  pl.BlockSpec((dq, dq), lambda b, j: (0, 0))],
        out_specs=[pl.BlockSpec((1, 1, dq, ATTN_TILE), lambda b, j: (b, j, 0, 0)),
                   pl.BlockSpec((ATTN_TILE, dq), tok),
                   pl.BlockSpec((1, 1, dq, ATTN_TILE), lambda b, j: (b, j, 0, 0)),
                   pl.BlockSpec((ATTN_TILE, LANES), tok)],
        out_shape=[jax.ShapeDtypeStruct((batch, nt, dq, ATTN_TILE), BF16),
                   jax.ShapeDtypeStruct((t, dq), BF16),
                   jax.ShapeDtypeStruct((batch, nt, dq, ATTN_TILE), BF16),
                   jax.ShapeDtypeStruct((t, LANES), F32)],
        scratch_shapes=[pltpu.VMEM((1, LANES), F32)],
        compiler_params=_params("parallel", "arbitrary"),
        name="fox_prep",
    )(proj, proj, proj, fx, fb, qg, kg, ones)


def _fox_attn_kernel(qt_ref, k_ref, vt_ref, f_ref, y_ref, qm_ref, lg_ref, m_ref, l_ref, acc_ref, *,
                     heads, head_dim):
    tq = y_ref.shape[0]
    i = pl.program_id(1)
    pair = LANES // head_dim
    feat = lax.broadcasted_iota(I32, (LANES, tq), 0)
    key_i = lax.broadcasted_iota(I32, (tq, tq), 0)
    qry_i = lax.broadcasted_iota(I32, (tq, tq), 1)

    for h in range(heads):
        p, s = divmod(h, pair)
        q2t = qt_ref[0, 0, p * LANES:(p + 1) * LANES, :]
        qm_ref[h] = jnp.where((feat >= s * head_dim) & (feat < (s + 1) * head_dim), q2t, jnp.zeros_like(q2t))
    m_ref[...] = jnp.full(m_ref.shape, NEG_BIG, F32)
    l_ref[...] = jnp.zeros(l_ref.shape, F32)
    acc_ref[...] = jnp.zeros(acc_ref.shape, F32)

    def tile(j, masked):
        start = pl.multiple_of(j * tq, tq)
        fk = f_ref[pl.ds(start, tq), :]
        for h in range(heads):
            p = h // pair
            k2 = k_ref[pl.ds(start, tq), p * LANES:(p + 1) * LANES]
            lg_ref[h] = jnp.dot(k2, qm_ref[h], preferred_element_type=F32)
        for h in range(heads):
            vt = vt_ref[0, j, h * head_dim:(h + 1) * head_dim, :]
            logit = lg_ref[h] - fk[:, h:h + 1]
            if masked:
                logit = jnp.where(key_i <= qry_i, logit, NEG_BIG)
            m_old = m_ref[h][0:1, :]
            m_new = jnp.maximum(m_old, jnp.max(logit, axis=0, keepdims=True))
            alpha = jnp.exp2(m_old - m_new)
            pr = jnp.exp2(logit - m_new)
            l_new = alpha * l_ref[h][0:1, :] + jnp.sum(pr, axis=0, keepdims=True)
            acc_ref[h] = alpha * acc_ref[h] + jnp.dot(vt, pr.astype(BF16), preferred_element_type=F32)
            m_ref[h] = jnp.broadcast_to(m_new, (SUBLANES, tq))
            l_ref[h] = jnp.broadcast_to(l_new, (SUBLANES, tq))

    def body(j, carry):
        tile(j, False)
        return carry

    lax.fori_loop(0, i, body, 0)
    tile(i, True)

    for p in range(heads // pair):
        out_t = jnp.concatenate([acc_ref[h] / l_ref[h][0:1, :] for h in range(p * pair, (p + 1) * pair)], axis=0)
        y_ref[:, p * LANES:(p + 1) * LANES] = out_t.T.astype(y_ref.dtype)


def _fox_attn(qt, kn, vt, f, batch, seq, heads, head_dim):
    t, dq = kn.shape
    nq = seq // ATTN_TILE
    return pl.pallas_call(
        functools.partial(_fox_attn_kernel, heads=heads, head_dim=head_dim),
        grid=(batch, nq),
        in_specs=[pl.BlockSpec((1, 1, dq, ATTN_TILE), lambda b, i: (b, i, 0, 0)),
                  pl.BlockSpec((seq, dq), lambda b, i: (b, 0)),
                  pl.BlockSpec((1, nq, dq, ATTN_TILE), lambda b, i: (b, 0, 0, 0)),
                  pl.BlockSpec((seq, LANES), lambda b, i: (b, 0))],
        out_specs=pl.BlockSpec((ATTN_TILE, dq), lambda b, i: (b * nq + i, 0)),
        out_shape=jax.ShapeDtypeStruct((t, dq), BF16),
        scratch_shapes=[pltpu.VMEM((heads, LANES, ATTN_TILE), BF16),
                        pltpu.VMEM((heads, ATTN_TILE, ATTN_TILE), F32),
                        pltpu.VMEM((heads, SUBLANES, ATTN_TILE), F32),
                        pltpu.VMEM((heads, SUBLANES, ATTN_TILE), F32),
                        pltpu.VMEM((heads, head_dim, ATTN_TILE), F32)],
        compiler_params=_params("parallel", "arbitrary"),
        name="fox_attn",
    )(qt, kn, vt, f)


def _gla_kernel(q_ref, k_ref, v_ref, r_ref, gx_ref, w2_ref, gb_ref, og_ref, y_ref, st_ref, *, heads, scale):
    rows = q_ref.shape[0]
    dk = q_ref.shape[1] // heads
    dv = v_ref.shape[1] // heads
    ck = GLA_CHUNK

    @pl.when(pl.program_id(1) == 0)
    def _():
        st_ref[...] = jnp.zeros_like(st_ref)

    g = jnp.dot(gx_ref[...].astype(BF16), w2_ref[...], preferred_element_type=F32) + gb_ref[...]
    log_g = -_softplus(-g) * (1.0 / GATE_TAU)
    rowc = lax.broadcasted_iota(I32, (ck, ck), 0)
    colc = lax.broadcasted_iota(I32, (ck, ck), 1)

    for c in range(rows // ck):
        rs = slice(c * ck, (c + 1) * ck)
        for h in range(heads):
            ks = slice(h * dk, (h + 1) * dk)
            vs = slice(h * dv, (h + 1) * dv)
            cum = _cumsum_rows(log_g[rs, ks])
            last = cum[ck - 1:ck, :]
            kf = k_ref[rs, ks].astype(F32)
            q_b = (q_ref[rs, ks].astype(F32) * scale * jnp.exp(cum)).astype(BF16)
            k_b = (kf * jnp.exp(-cum)).astype(BF16)
            ke_b = (kf * jnp.exp(last - cum)).astype(BF16)
            v = v_ref[rs, vs]
            intra = lax.dot_general(q_b, k_b, (((1,), (1,)), ((), ())), preferred_element_type=F32)
            intra = jnp.where(colc <= rowc, intra, 0.0)
            state = st_ref[h]
            o = (jnp.dot(intra.astype(BF16), v, preferred_element_type=F32)
                 + lax.dot_general(q_b, state.astype(BF16), (((1,), (1,)), ((), ())),
                                   preferred_element_type=F32))
            st_ref[h] = state * jnp.exp(last) + lax.dot_general(
                v, ke_b, (((0,), (0,)), ((), ())), preferred_element_type=F32)
            ms = jnp.mean(o * o, axis=-1, keepdims=True)
            o = o * lax.rsqrt(ms + EPS) * og_ref[...]
            rg = r_ref[rs, vs].astype(F32)
            y_ref[rs, vs] = (o * (rg * _sigmoid(rg))).astype(y_ref.dtype)


def _gla(proj, gx, gate_w2, gate_b, out_gain, batch, seq, heads):
    t = proj.shape[0]
    rank, d_k = gate_w2.shape
    d_v = heads * out_gain.shape[0]
    nt = seq // SEQ_TILE
    kb = d_k // d_k
    w2 = jnp.zeros((LANES, d_k), F32).at[:rank].set(gate_w2).astype(BF16)
    tok = lambda b, j: (b * nt + j, 0)
    vblk = 2 * d_k // d_v
    return pl.pallas_call(
        functools.partial(_gla_kernel, heads=heads, scale=(d_k // heads) ** -0.5),
        grid=(batch, nt),
        in_specs=[pl.BlockSpec((SEQ_TILE, d_k), lambda b, j: (b * nt + j, 0)),
                  pl.BlockSpec((SEQ_TILE, d_k), lambda b, j: (b * nt + j, kb)),
                  pl.BlockSpec((SEQ_TILE, d_v), lambda b, j: (b * nt + j, vblk)),
                  pl.BlockSpec((SEQ_TILE, d_v), lambda b, j: (b * nt + j, vblk + 1)),
                  pl.BlockSpec((SEQ_TILE, LANES), tok),
                  pl.BlockSpec((LANES, d_k), lambda b, j: (0, 0)),
                  pl.BlockSpec((1, d_k), lambda b, j: (0, 0)),
                  pl.BlockSpec((1, d_v // heads), lambda b, j: (0, 0))],
        out_specs=pl.BlockSpec((SEQ_TILE, d_v), tok),
        out_shape=jax.ShapeDtypeStruct((t, d_v), BF16),
        scratch_shapes=[pltpu.VMEM((heads, d_v // heads, d_k // heads), F32)],
        compiler_params=_params("parallel", "arbitrary"),
        name="gla",
    )(proj, proj, proj, proj, gx, w2, gate_b[None, :], out_gain[None, :])


def _route_kernel(x_ref, gain_ref, sc_ref, sh_ref, rw_ref, rb_ref, tri_ref,
                  h_ref, eidx_ref, gw_ref, rank_ref, cnt_ref, *, n_groups):
    n_exp = rw_ref.shape[0]
    epg = n_exp // n_groups
    tm = x_ref.shape[0]

    @pl.when(pl.program_id(0) == 0)
    def _():
        cnt_ref[...] = jnp.zeros_like(cnt_ref)

    h = _rms_mod(x_ref[...], gain_ref[...], sc_ref[0], sh_ref[0])
    h_ref[...] = h.astype(h_ref.dtype)
    logits = lax.dot_general(rw_ref[...], h, (((1,), (1,)), ((), ())),
                             preferred_element_type=F32, precision=HIGHEST)
    scores = _sigmoid(logits)
    sel = scores + rb_ref[...]

    sub = lax.broadcasted_iota(I32, (epg, tm), 0)

    def top2(v):
        m1 = jnp.max(v, axis=0, keepdims=True)
        i1 = jnp.min(jnp.where(v == m1, sub, epg), axis=0, keepdims=True)
        v2 = jnp.where(sub == i1, -jnp.inf, v)
        m2 = jnp.max(v2, axis=0, keepdims=True)
        i2 = jnp.min(jnp.where(v2 == m2, sub, epg), axis=0, keepdims=True)
        return m1, i1, m2, i2

    groups = [sel[g * epg:(g + 1) * epg, :] for g in range(n_groups)]
    best = None
    for g in range(n_groups):
        m1, _, m2, _ = top2(groups[g])
        gs = m1 + m2
        if g == 0:
            best, g_idx = gs, jnp.zeros((1, tm), I32)
        else:
            better = gs > best
            g_idx = jnp.where(better, g, g_idx)
            best = jnp.where(better, gs, best)
    in_group = groups[0]
    sc_group = scores[0:epg, :]
    for g in range(1, n_groups):
        in_group = jnp.where(g_idx == g, groups[g], in_group)
        sc_group = jnp.where(g_idx == g, scores[g * epg:(g + 1) * epg, :], sc_group)
    _, i1, _, i2 = top2(in_group)
    w0 = jnp.sum(jnp.where(sub == i1, sc_group, 0.0), axis=0, keepdims=True)
    w1 = jnp.sum(jnp.where(sub == i2, sc_group, 0.0), axis=0, keepdims=True)
    wsum = w0 + w1
    e0 = g_idx * epg + i1
    e1 = g_idx * epg + i2
    eidx_ref[0:1, :] = e0
    eidx_ref[1:2, :] = e1
    gw_ref[0:1, :] = w0 / wsum
    gw_ref[1:2, :] = w1 / wsum

    erow = lax.broadcasted_iota(I32, (n_exp, tm), 0)
    oh0 = erow == e0
    oh1 = erow == e1
    taken = jnp.where(oh0 | oh1, 1.0, 0.0)
    before = jnp.dot(taken.astype(BF16), tri_ref[...], preferred_element_type=F32) + cnt_ref[:, 0:1]
    r0 = jnp.sum(jnp.where(oh0, before, 0.0), axis=0, keepdims=True)
    r1 = jnp.sum(jnp.where(oh1, before, 0.0), axis=0, keepdims=True)
    rank_ref[0:1, :] = r0.astype(I32)
    rank_ref[1:2, :] = r1.astype(I32)
    cnt_ref[...] = cnt_ref[...] + jnp.sum(taken, axis=1, keepdims=True)


def _route(x2, gain, sc, sh, router_w, router_b, seq):
    t, d = x2.shape
    n_exp = router_w.shape[1]
    tps = seq // TOKEN_TILE
    pos = jnp.arange(TOKEN_TILE)
    tri = (pos[:, None] < pos[None, :]).astype(BF16)
    tokl = lambda i: (0, i)
    return pl.pallas_call(
        functools.partial(_route_kernel, n_groups=N_GROUPS),
        grid=(t // TOKEN_TILE,),
        in_specs=[pl.BlockSpec((TOKEN_TILE, d), lambda i: (i, 0)),
                  pl.BlockSpec((1, d), lambda i: (0, 0)),
                  pl.BlockSpec((1, 1, d), lambda i: (i // tps, 0, 0)),
                  pl.BlockSpec((1, 1, d), lambda i: (i // tps, 0, 0)),
                  pl.BlockSpec((n_exp, d), lambda i: (0, 0)),
                  pl.BlockSpec((n_exp, 1), lambda i: (0, 0)),
                  pl.BlockSpec((TOKEN_TILE, TOKEN_TILE), lambda i: (0, 0))],
        out_specs=[pl.BlockSpec((TOKEN_TILE, d), lambda i: (i, 0)),
                   pl.BlockSpec((TOP_K, TOKEN_TILE), tokl),
                   pl.BlockSpec((TOP_K, TOKEN_TILE), tokl),
                   pl.BlockSpec((TOP_K, TOKEN_TILE), tokl),
                   pl.BlockSpec((n_exp, LANES), lambda i: (0, 0))],
        out_shape=[jax.ShapeDtypeStruct((t, d), F32),
                   jax.ShapeDtypeStruct((TOP_K, t), I32),
                   jax.ShapeDtypeStruct((TOP_K, t), F32),
                   jax.ShapeDtypeStruct((TOP_K, t), I32),
                   jax.ShapeDtypeStruct((n_exp, LANES), F32)],
        compiler_params=_params("arbitrary"),
        name="moe_route",
    )(x2, gain, sc, sh, router_w.T, router_b[:, None], tri)


def _row_copy(table_hbm, row, buf, slot, r, sem):
    return pltpu.make_async_copy(table_hbm.at[pl.ds(row, 1), :], buf.at[slot, pl.ds(r, 1), :], sem.at[slot])


def _expert_kernel(blk_e_ref, nact_ref, src_ref, h_hbm, w1_ref, w3_ref, w2_ref, o_ref,
                   xbuf, sem, w1b, w3b, w2b):
    b = pl.program_id(0)
    n_act = nact_ref[0]
    rows = xbuf.shape[1]

    def start_gather(blk, slot):
        def issue(r, carry):
            _row_copy(h_hbm, src_ref[blk * rows + r], xbuf, slot, r, sem).start()
            return carry
        lax.fori_loop(0, rows, issue, 0, unroll=DMA_UNROLL)

    @pl.when(b == 0)
    def _():
        start_gather(0, 0)

    @pl.when(b + 1 < n_act)
    def _():
        start_gather(b + 1, (b + 1) % 2)

    @pl.when(b < n_act)
    def _():
        slot = b % 2

        def wait_row(r, carry):
            _row_copy(h_hbm, 0, xbuf, slot, r, sem).wait()
            return carry
        lax.fori_loop(0, rows, wait_row, 0, unroll=DMA_UNROLL)

        changed = blk_e_ref[b] != blk_e_ref[jnp.maximum(b - 1, 0)]

        @pl.when((b == 0) | changed)
        def _():
            w1b[...] = w1_ref[0, 0].astype(BF16)
            w3b[...] = w3_ref[0, 0].astype(BF16)
            w2b[...] = w2_ref[0, 0].astype(BF16)

        xb = xbuf[slot].astype(BF16)
        a = jnp.dot(xb, w1b[...], preferred_element_type=F32)
        hid = (a * _sigmoid(a)) * jnp.dot(xb, w3b[...], preferred_element_type=F32)
        o_ref[...] = jnp.dot(hid.astype(BF16), w2b[...], preferred_element_type=F32)

    @pl.when(b >= n_act)
    def _():
        o_ref[...] = jnp.zeros_like(o_ref)


def _experts(h, src, blk_e, n_active, w1, w3, w2, layer):
    d = h.shape[1]
    p_len = src.shape[0]
    de = w1.shape[-1]
    n_blocks = p_len // EXPERT_ROWS

    def last_active(b, nact_ref):
        return jnp.maximum(jnp.minimum(b, nact_ref[0] - 1), 0)

    def row_map(b, blk_e_ref, nact_ref, src_ref):
        return (b, 0)

    def w_map(b, blk_e_ref, nact_ref, src_ref):
        return (layer, blk_e_ref[last_active(b, nact_ref)], 0, 0)

    return pl.pallas_call(
        _expert_kernel,
        grid_spec=pltpu.PrefetchScalarGridSpec(
            num_scalar_prefetch=3,
            grid=(n_blocks,),
            in_specs=[pl.BlockSpec(memory_space=pl.ANY),
                      pl.BlockSpec((1, 1, d, de), w_map),
                      pl.BlockSpec((1, 1, d, de), w_map),
                      pl.BlockSpec((1, 1, de, d), w_map)],
            out_specs=pl.BlockSpec((EXPERT_ROWS, d), row_map),
            scratch_shapes=[pltpu.VMEM((2, EXPERT_ROWS, d), F32), pltpu.SemaphoreType.DMA((2,)),
                            pltpu.VMEM((d, de), BF16), pltpu.VMEM((d, de), BF16), pltpu.VMEM((de, d), BF16)]),
        out_shape=jax.ShapeDtypeStruct((p_len, d), F32),
        compiler_params=_params("arbitrary"),
        name="moe_experts",
    )(blk_e, n_active, src, h, w1, w3, w2)


def _combine_kernel(dest_ref, x_ref, g_ref, gw_ref, y_hbm, o_ref, ybuf, sem):
    i = pl.program_id(0)
    n_tiles = pl.num_programs(0)
    tm = x_ref.shape[0]
    n_tok = n_tiles * tm

    def start_gather(tile, slot):
        for k in range(TOP_K):
            def issue(r, carry):
                _row_copy(y_hbm, dest_ref[k * n_tok + tile * tm + r], ybuf, slot, k * tm + r, sem).start()
                return carry
            lax.fori_loop(0, tm, issue, 0, unroll=DMA_UNROLL)

    @pl.when(i == 0)
    def _():
        start_gather(0, 0)

    @pl.when(i + 1 < n_tiles)
    def _():
        start_gather(i + 1, (i + 1) % 2)

    slot = i % 2

    def wait_row(r, carry):
        _row_copy(y_hbm, 0, ybuf, slot, r, sem).wait()
        return carry
    lax.fori_loop(0, TOP_K * tm, wait_row, 0, unroll=DMA_UNROLL)

    moe = ybuf[slot, 0:tm, :] * gw_ref[:, 0:1]
    for k in range(1, TOP_K):
        moe = moe + ybuf[slot, k * tm:(k + 1) * tm, :] * gw_ref[:, k:k + 1]
    o_ref[...] = x_ref[...] + g_ref[0] * moe


def _combine(x2, g, y_pad, dest, gw_t, seq):
    t, d = x2.shape
    tps = seq // TOKEN_TILE
    tok = lambda i, dest_ref: (i, 0)
    return pl.pallas_call(
        _combine_kernel,
        grid_spec=pltpu.PrefetchScalarGridSpec(
            num_scalar_prefetch=1,
            grid=(t // TOKEN_TILE,),
            in_specs=[pl.BlockSpec((TOKEN_TILE, d), tok),
                      pl.BlockSpec((1, 1, d), lambda i, dest_ref: (i // tps, 0, 0)),
                      pl.BlockSpec((TOKEN_TILE, TOP_K), tok),
                      pl.BlockSpec(memory_space=pl.ANY)],
            out_specs=pl.BlockSpec((TOKEN_TILE, d), tok),
            scratch_shapes=[pltpu.VMEM((2, TOP_K * TOKEN_TILE, d), F32), pltpu.SemaphoreType.DMA((2,))]),
        out_shape=jax.ShapeDtypeStruct((t, d), F32),
        compiler_params=_params("arbitrary"),
        name="moe_combine",
    )(dest.reshape(-1), x2, g, gw_t, y_pad)


def _moe(x2, gain, sc, sh, g, router_w, router_b, w1, w3, w2, layer, seq):
    t, d = x2.shape
    n_exp = router_w.shape[1]
    h, e_idx, gw, rank, cnt = _route(x2, gain, sc, sh, router_w, router_b, seq)
    counts = cnt[:, 0].astype(I32)
    padded = (counts + EXPERT_ROWS - 1) // EXPERT_ROWS * EXPERT_ROWS
    pend = jnp.cumsum(padded)
    pstart = pend - padded
    dest = pstart[e_idx] + rank
    n_blocks = -(-(t * TOP_K) // EXPERT_ROWS) + n_exp
    p_len = n_blocks * EXPERT_ROWS
    blk_start = jnp.arange(n_blocks, dtype=I32) * EXPERT_ROWS
    blk_e = jnp.minimum(jnp.sum((pend[None, :] <= blk_start[:, None]).astype(I32), axis=1), n_exp - 1)
    n_active = (pend[-1:] // EXPERT_ROWS).astype(I32)
    tok_ids = jnp.broadcast_to(jnp.arange(t, dtype=I32)[None, :], dest.shape)
    src = jnp.zeros((p_len,), I32).at[dest.reshape(-1)].set(tok_ids.reshape(-1))
    y_pad = _experts(h, src, blk_e, n_active, w1, w3, w2, layer)
    return _combine(x2, g, y_pad, dest, gw.T, seq)


def _block_diag(w):
    n, a, b = w.shape
    eye = jnp.eye(n, dtype=w.dtype)
    return (eye[:, None, :, None] * w[:, :, None, :]).reshape(n * a, n * b)


def kernel(x, c, router_w, router_b, ada_w, ada_b, norm_mix, norm_ffn, ev_w_in, ev_conv_w, ev_conv_b, ev_r_w, ev_r_b, ev_i_w, ev_i_b, ev_lam, ev_forget_b, ev_q_gain, ev_k_gain, ev_w_out, od_w_in, od_gate_w2, od_gate_b, od_out_gain, od_w_out, moe_w1, moe_w3, moe_w2):
    batch, seq, d = x.shape
    depth = ada_w.shape[0]
    x2 = x.reshape(batch * seq, d)
    mods = _ada(c, ada_w, ada_b)

    for layer in range(depth):
        sh_m, sc_m, g_m, sh_f, sc_f, g_f = [mods[layer, :, None, k * d:(k + 1) * d] for k in range(6)]
        j = layer // 2
        if layer % 2 == 0:
            d_rnn = ev_conv_w.shape[-1]
            heads, head_dim = ev_forget_b.shape[-1], ev_q_gain.shape[-1]
            d_fox = heads * head_dim
            n_main = 2 * d_rnn + 3 * d_fox
            w_in = ev_w_in[j]
            wx = jnp.zeros((d, LANES), F32).at[:, :heads].set(w_in[:, n_main:]).astype(BF16)
            proj, fx = _in_proj(x2, norm_mix[layer][None, :], sc_m, sh_m, w_in[:, :n_main].astype(BF16), wx, seq)
            wg = jnp.concatenate([_block_diag(ev_r_w[j]), _block_diag(ev_i_w[j])], axis=1).astype(BF16)
            bg = jnp.concatenate([ev_r_b[j].reshape(-1), ev_i_b[j].reshape(-1)])[None, :]
            y_rnn = _rglru(proj, ev_conv_w[j], ev_conv_b[j][None, :], wg, bg, ev_lam[j][None, :], batch, seq)
            qt, kn, vt, f = _fox_prep(proj, fx, ev_forget_b[j], ev_q_gain[j], ev_k_gain[j],
                                      batch, seq, heads, head_dim)
            y_fox = _fox_attn(qt, kn, vt, f, batch, seq, heads, head_dim)
            w_out = ev_w_out[j].astype(BF16)
            x2 = _out_proj([y_rnn, y_fox], [w_out[:d_rnn], w_out[d_rnn:]], x2, g_m, seq)
        else:
            rank, d_k = od_gate_w2.shape[1:]
            d_v = od_w_out.shape[1]
            heads = d_v // od_out_gain.shape[-1]
            n_main = 2 * d_k + 2 * d_v
            w_in = od_w_in[j]
            wx = jnp.zeros((d, LANES), F32).at[:, :rank].set(w_in[:, n_main:]).astype(BF16)
            proj, gx = _in_proj(x2, norm_mix[layer][None, :], sc_m, sh_m, w_in[:, :n_main].astype(BF16), wx, seq)
            y = _gla(proj, gx, od_gate_w2[j], od_gate_b[j], od_out_gain[j], batch, seq, heads)
            x2 = _out_proj([y], [od_w_out[j].astype(BF16)], x2, g_m, seq)
        x2 = _moe(x2, norm_ffn[layer][None, :], sc_f, sh_f, g_f, router_w, router_b,
                  moe_w1, moe_w3, moe_w2, layer, seq)
    return x2.reshape(batch, seq, d)
```

```python
import functools

import jax
import jax.numpy as jnp
from jax import lax
from jax.experimental import pallas as pl
from jax.experimental.pallas import tpu as pltpu

F32 = jnp.float32
BF16 = jnp.bfloat16
I32 = jnp.int32
HIGHEST = lax.Precision.HIGHEST

EPS = 1e-6
RGLRU_C = 8.0
CONV_W = 4
GATE_TAU = 16.0
GLA_CHUNK = 64
N_GROUPS = 4
TOP_K = 2
NEG_BIG = -1e30
LOG2E = 1.4426950408889634

LANES = 128
SUBLANES = 8
TOKEN_TILE = 512
SEQ_TILE = 256
ATTN_TILE = 256
EXPERT_ROWS = 256
DMA_UNROLL = 16
VMEM_LIMIT = 48 * 1024 * 1024


def _params(*sem):
    return pltpu.CompilerParams(dimension_semantics=sem, vmem_limit_bytes=VMEM_LIMIT)


def _rms_mod(x, gain, sc, sh):
    ms = jnp.mean(x * x, axis=-1, keepdims=True)
    return (x * lax.rsqrt(ms + EPS) * gain) * (1.0 + sc) + sh


def _softplus(z):
    return jnp.maximum(z, 0.0) + jnp.log1p(jnp.exp(-jnp.abs(z)))


def _sigmoid(z):
    return 1.0 / (1.0 + jnp.exp(-z))


def _shift_rows(v, d, fill, row):
    return jnp.where(row >= d, pltpu.roll(v, d, axis=0), fill)


def _cumsum_rows(v):
    n = v.shape[0]
    row = lax.broadcasted_iota(I32, v.shape, 0)
    d = 1
    while d < n:
        v = v + _shift_rows(v, d, 0.0, row)
        d *= 2
    return v


def _ada_kernel(c_ref, w_ref, b_ref, o_ref):
    c = c_ref[...]
    o_ref[0] = jnp.dot(c * _sigmoid(c), w_ref[0], preferred_element_type=F32, precision=HIGHEST) + b_ref[0]


def _ada(c, ada_w, ada_b):
    depth, d, n = ada_w.shape
    b = c.shape[0]
    tn = n // 4
    return pl.pallas_call(
        _ada_kernel,
        grid=(depth, n // tn),
        in_specs=[pl.BlockSpec((b, d), lambda l, j: (0, 0)),
                  pl.BlockSpec((1, d, tn), lambda l, j: (l, 0, j)),
                  pl.BlockSpec((1, 1, tn), lambda l, j: (l, 0, j))],
        out_specs=pl.BlockSpec((1, b, tn), lambda l, j: (l, 0, j)),
        out_shape=jax.ShapeDtypeStruct((depth, b, n), F32),
        compiler_params=_params("parallel", "parallel"),
        name="ada_mod",
    )(c, ada_w, ada_b.reshape(depth, 1, n))


def _in_proj_kernel(x_ref, gain_ref, sc_ref, sh_ref, w_ref, wx_ref, o_ref, ox_ref, *, n_chunk):
    h = _rms_mod(x_ref[...], gain_ref[...], sc_ref[0], sh_ref[0]).astype(BF16)
    for c in range(0, o_ref.shape[1], n_chunk):
        o_ref[:, c:c + n_chunk] = jnp.dot(h, w_ref[:, c:c + n_chunk],
                                          preferred_element_type=F32).astype(o_ref.dtype)
    ox_ref[...] = jnp.dot(h, wx_ref[...], preferred_element_type=F32)


def _in_proj(x2, gain, sc, sh, w, wx, seq):
    t, d = x2.shape
    n = w.shape[1]
    tps = seq // TOKEN_TILE
    return pl.pallas_call(
        functools.partial(_in_proj_kernel, n_chunk=512),
        grid=(t // TOKEN_TILE,),
        in_specs=[pl.BlockSpec((TOKEN_TILE, d), lambda i: (i, 0)),
                  pl.BlockSpec((1, d), lambda i: (0, 0)),
                  pl.BlockSpec((1, 1, d), lambda i: (i // tps, 0, 0)),
                  pl.BlockSpec((1, 1, d), lambda i: (i // tps, 0, 0)),
                  pl.BlockSpec((d, n), lambda i: (0, 0)),
                  pl.BlockSpec((d, LANES), lambda i: (0, 0))],
        out_specs=[pl.BlockSpec((TOKEN_TILE, n), lambda i: (i, 0)),
                   pl.BlockSpec((TOKEN_TILE, LANES), lambda i: (i, 0))],
        out_shape=[jax.ShapeDtypeStruct((t, n), BF16), jax.ShapeDtypeStruct((t, LANES), F32)],
        compiler_params=_params("parallel"),
        name="in_proj",
    )(x2, gain, sc, sh, w, wx)


def _out_proj_kernel(*refs, n_parts):
    y_refs, w_refs = refs[:n_parts], refs[n_parts:2 * n_parts]
    x_ref, g_ref, o_ref = refs[2 * n_parts:]
    acc = jnp.dot(y_refs[0][...], w_refs[0][...], preferred_element_type=F32)
    for y_ref, w_ref in zip(y_refs[1:], w_refs[1:]):
        acc = acc + jnp.dot(y_ref[...], w_ref[...], preferred_element_type=F32)
    o_ref[...] = x_ref[...] + g_ref[0] * acc


def _out_proj(ys, ws, x2, g, seq):
    t, d = x2.shape
    tps = seq // TOKEN_TILE
    n_parts = len(ys)
    in_specs = [pl.BlockSpec((TOKEN_TILE, y.shape[1]), lambda i: (i, 0)) for y in ys]
    in_specs += [pl.BlockSpec(w.shape, lambda i: (0, 0)) for w in ws]
    in_specs += [pl.BlockSpec((TOKEN_TILE, d), lambda i: (i, 0)),
                 pl.BlockSpec((1, 1, d), lambda i: (i // tps, 0, 0))]
    return pl.pallas_call(
        functools.partial(_out_proj_kernel, n_parts=n_parts),
        grid=(t // TOKEN_TILE,),
        in_specs=in_specs,
        out_specs=pl.BlockSpec((TOKEN_TILE, d), lambda i: (i, 0)),
        out_shape=jax.ShapeDtypeStruct((t, d), F32),
        compiler_params=_params("parallel"),
        name="out_proj",
    )(*ys, *ws, x2, g)


def _rglru_kernel(x_ref, gate_ref, cw_ref, cb_ref, wg_ref, bg_ref, lam_ref, y_ref, ext_ref, hc_ref):
    rows, ch = x_ref.shape

    @pl.when(pl.program_id(1) == 0)
    def _():
        ext_ref[0:SUBLANES, :] = jnp.zeros((SUBLANES, ch), F32)
        hc_ref[...] = jnp.zeros_like(hc_ref)

    x = x_ref[...].astype(F32)
    ext_ref[SUBLANES:SUBLANES + rows, :] = x
    xc = cb_ref[...] + cw_ref[CONV_W - 1:CONV_W, :] * x
    for j in range(CONV_W - 1):
        off = SUBLANES - (CONV_W - 1) + j
        xc = xc + cw_ref[j:j + 1, :] * ext_ref[off:off + rows, :]
    ext_ref[0:SUBLANES, :] = x[rows - SUBLANES:, :]

    gates = jnp.dot(xc.astype(BF16), wg_ref[...], preferred_element_type=F32) + bg_ref[...]
    r = _sigmoid(gates[:, :ch])
    gi = _sigmoid(gates[:, ch:])
    log_a = (-RGLRU_C * _softplus(-lam_ref[...])) * r
    a = jnp.exp(log_a)
    u = jnp.sqrt(1.0 - a * a) * (gi * xc)

    row = lax.broadcasted_iota(I32, (rows, ch), 0)
    d = 1
    while d < rows:
        u = u + a * _shift_rows(u, d, 0.0, row)
        a = a * _shift_rows(a, d, 1.0, row)
        d *= 2
    h = u + a * hc_ref[...]
    hc_ref[...] = h[rows - 1:rows, :]
    y_ref[...] = (h * jax.nn.gelu(gate_ref[...].astype(F32))).astype(y_ref.dtype)


def _rglru(proj, conv_w, conv_b, wg, bg, lam, batch, seq):
    t = proj.shape[0]
    ch = conv_w.shape[1]
    nt = seq // SEQ_TILE
    return pl.pallas_call(
        _rglru_kernel,
        grid=(batch, nt),
        in_specs=[pl.BlockSpec((SEQ_TILE, ch), lambda b, j: (b * nt + j, 0)),
                  pl.BlockSpec((SEQ_TILE, ch), lambda b, j: (b * nt + j, 1)),
                  pl.BlockSpec((CONV_W, ch), lambda b, j: (0, 0)),
                  pl.BlockSpec((1, ch), lambda b, j: (0, 0)),
                  pl.BlockSpec((ch, 2 * ch), lambda b, j: (0, 0)),
                  pl.BlockSpec((1, 2 * ch), lambda b, j: (0, 0)),
                  pl.BlockSpec((1, ch), lambda b, j: (0, 0))],
        out_specs=pl.BlockSpec((SEQ_TILE, ch), lambda b, j: (b * nt + j, 0)),
        out_shape=jax.ShapeDtypeStruct((t, ch), BF16),
        scratch_shapes=[pltpu.VMEM((SEQ_TILE + SUBLANES, ch), F32), pltpu.VMEM((1, ch), F32)],
        compiler_params=_params("parallel", "arbitrary"),
        name="rglru",
    )(proj, proj, conv_w, conv_b, wg, bg, lam)


def _fox_prep_kernel(q_ref, k_ref, v_ref, fx_ref, fb_ref, qg_ref, kg_ref, ones_ref, qt_ref, kn_ref, vt_ref, f_ref,
                     fc_ref, *, scale):
    @pl.when(pl.program_id(1) == 0)
    def _():
        fc_ref[...] = jnp.zeros_like(fc_ref)

    def qk_norm(v_ref, g_ref, mult):
        v = v_ref[...].astype(F32)
        ms = jnp.dot((v * v).astype(BF16), ones_ref[...], preferred_element_type=F32)
        return (v * lax.rsqrt(ms + EPS) * g_ref[...]) * mult

    qt_ref[0, 0] = qk_norm(q_ref, qg_ref, scale * LOG2E).T.astype(qt_ref.dtype)
    kn_ref[...] = qk_norm(k_ref, kg_ref, 1.0).astype(kn_ref.dtype)
    vt_ref[0, 0] = v_ref[...].astype(F32).T.astype(vt_ref.dtype)

    log_f = -_softplus(-(fx_ref[...] + fb_ref[...])) * LOG2E
    f = _cumsum_rows(log_f) + fc_ref[...]
    fc_ref[...] = f[f.shape[0] - 1:, :]
    f_ref[...] = f


def _fox_prep(proj, fx, forget_b, q_gain, k_gain, batch, seq, heads, head_dim):
    t = proj.shape[0]
    dq = heads * head_dim
    nt = seq // ATTN_TILE
    qblk = (proj.shape[1] - 3 * dq) // dq
    fb = jnp.zeros((1, LANES), F32).at[0, :heads].set(forget_b)
    qg = jnp.tile(q_gain, heads)[None, :]
    kg = jnp.tile(k_gain, heads)[None, :]
    head_of = jnp.arange(dq) // head_dim
    ones = (head_of[:, None] == head_of[None, :]).astype(BF16) * (1.0 / head_dim)
    tok = lambda b, j: (b * nt + j, 0)
    return pl.pallas_call(
        functools.partial(_fox_prep_kernel, scale=head_dim ** -0.5),
        grid=(batch, nt),
        in_specs=[pl.BlockSpec((ATTN_TILE, dq), lambda b, j: (b * nt + j, qblk)),
                  pl.BlockSpec((ATTN_TILE, dq), lambda b, j: (b * nt + j, qblk + 1)),
                  pl.BlockSpec((ATTN_TILE, dq), lambda b, j: (b * nt + j, qblk + 2)),
                  pl.BlockSpec((ATTN_TILE, LANES), tok),
                  pl.BlockSpec((1, LANES), lambda b, j: (0, 0)),
                  pl.BlockSpec((1, dq), lambda b, j: (0, 0)),
                  pl.BlockSpec((1, dq), lambda b, j: (0, 0)),
                  pl.BlockSpec((dq, dq), lambda b, j: (0, 0))],
        out_specs=[pl.BlockSpec((1, 1, dq, ATTN_TILE), lambda b, j: (b, j, 0, 0)),
                   pl.BlockSpec((ATTN_TILE, dq), tok),
                   pl.BlockSpec((1, 1, dq, ATTN_TILE), lambda b, j: (b, j, 0, 0)),
                   pl.BlockSpec((ATTN_TILE, LANES), tok)],
        out_shape=[jax.ShapeDtypeStruct((batch, nt, dq, ATTN_TILE), BF16),
                   jax.ShapeDtypeStruct((t, dq), BF16),
                   jax.ShapeDtypeStruct((batch, nt, dq, ATTN_TILE), BF16),
                   jax.ShapeDtypeStruct((t, LANES), F32)],
        scratch_shapes=[pltpu.VMEM((1, LANES), F32)],
        compiler_params=_params("parallel", "arbitrary"),
        name="fox_prep",
    )(proj, proj, proj, fx, fb, qg, kg, ones)


def _fox_attn_kernel(qt_ref, k_ref, vt_ref, f_ref, y_ref, qm_ref, lg_ref, m_ref, l_ref, acc_ref, *,
                     heads, head_dim):
    tq = y_ref.shape[0]
    i = pl.program_id(1)
    pair = LANES // head_dim
    feat = lax.broadcasted_iota(I32, (LANES, tq), 0)
    key_i = lax.broadcasted_iota(I32, (tq, tq), 0)
    qry_i = lax.broadcasted_iota(I32, (tq, tq), 1)

    for h in range(heads):
        p, s = divmod(h, pair)
        q2t = qt_ref[0, 0, p * LANES:(p + 1) * LANES, :]
        qm_ref[h] = jnp.where((feat >= s * head_dim) & (feat < (s + 1) * head_dim), q2t, jnp.zeros_like(q2t))
    m_ref[...] = jnp.full(m_ref.shape, NEG_BIG, F32)
    l_ref[...] = jnp.zeros(l_ref.shape, F32)
    acc_ref[...] = jnp.zeros(acc_ref.shape, F32)

    def tile(j, masked):
        start = pl.multiple_of(j * tq, tq)
        fk = f_ref[pl.ds(start, tq), :]
        for h in range(heads):
            p = h // pair
            k2 = k_ref[pl.ds(start, tq), p * LANES:(p + 1) * LANES]
            lg_ref[h] = jnp.dot(k2, qm_ref[h], preferred_element_type=F32)
        for h in range(heads):
            vt = vt_ref[0, j, h * head_dim:(h + 1) * head_dim, :]
            logit = lg_ref[h] - fk[:, h:h + 1]
            if masked:
                logit = jnp.where(key_i <= qry_i, logit, NEG_BIG)
            m_old = m_ref[h][0:1, :]
            m_new = jnp.maximum(m_old, jnp.max(logit, axis=0, keepdims=True))
            alpha = jnp.exp2(m_old - m_new)
            pr = jnp.exp2(logit - m_new)
            l_new = alpha * l_ref[h][0:1, :] + jnp.sum(pr, axis=0, keepdims=True)
            acc_ref[h] = alpha * acc_ref[h] + jnp.dot(vt, pr.astype(BF16), preferred_element_type=F32)
            m_ref[h] = jnp.broadcast_to(m_new, (SUBLANES, tq))
            l_ref[h] = jnp.broadcast_to(l_new, (SUBLANES, tq))

    def body(j, carry):
        tile(j, False)
        return carry

    lax.fori_loop(0, i, body, 0)
    tile(i, True)

    for p in range(heads // pair):
        out_t = jnp.concatenate([acc_ref[h] / l_ref[h][0:1, :] for h in range(p * pair, (p + 1) * pair)], axis=0)
        y_ref[:, p * LANES:(p + 1) * LANES] = out_t.T.astype(y_ref.dtype)


def _fox_attn(qt, kn, vt, f, batch, seq, heads, head_dim):
    t, dq = kn.shape
    nq = seq // ATTN_TILE
    return pl.pallas_call(
        functools.partial(_fox_attn_kernel, heads=heads, head_dim=head_dim),
        grid=(batch, nq),
        in_specs=[pl.BlockSpec((1, 1, dq, ATTN_TILE), lambda b, i: (b, i, 0, 0)),
                  pl.BlockSpec((seq, dq), lambda b, i: (b, 0)),
                  pl.BlockSpec((1, nq, dq, ATTN_TILE), lambda b, i: (b, 0, 0, 0)),
                  pl.BlockSpec((seq, LANES), lambda b, i: (b, 0))],
        out_specs=pl.BlockSpec((ATTN_TILE, dq), lambda b, i: (b * nq + i, 0)),
        out_shape=jax.ShapeDtypeStruct((t, dq), BF16),
        scratch_shapes=[pltpu.VMEM((heads, LANES, ATTN_TILE), BF16),
                        pltpu.VMEM((heads, ATTN_TILE, ATTN_TILE), F32),
                        pltpu.VMEM((heads, SUBLANES, ATTN_TILE), F32),
                        pltpu.VMEM((heads, SUBLANES, ATTN_TILE), F32),
                        pltpu.VMEM((heads, head_dim, ATTN_TILE), F32)],
        compiler_params=_params("parallel", "arbitrary"),
        name="fox_attn",
    )(qt, kn, vt, f)


def _gla_kernel(q_ref, k_ref, v_ref, r_ref, gx_ref, w2_ref, gb_ref, og_ref, y_ref, st_ref, *, heads, scale):
    rows = q_ref.shape[0]
    dk = q_ref.shape[1] // heads
    dv = v_ref.shape[1] // heads
    ck = GLA_CHUNK

    @pl.when(pl.program_id(1) == 0)
    def _():
        st_ref[...] = jnp.zeros_like(st_ref)

    g = jnp.dot(gx_ref[...].astype(BF16), w2_ref[...], preferred_element_type=F32) + gb_ref[...]
    log_g = -_softplus(-g) * (1.0 / GATE_TAU)
    rowc = lax.broadcasted_iota(I32, (ck, ck), 0)
    colc = lax.broadcasted_iota(I32, (ck, ck), 1)

    for c in range(rows // ck):
        rs = slice(c * ck, (c + 1) * ck)
        for h in range(heads):
            ks = slice(h * dk, (h + 1) * dk)
            vs = slice(h * dv, (h + 1) * dv)
            cum = _cumsum_rows(log_g[rs, ks])
            last = cum[ck - 1:ck, :]
            kf = k_ref[rs, ks].astype(F32)
            q_b = (q_ref[rs, ks].astype(F32) * scale * jnp.exp(cum)).astype(BF16)
            k_b = (kf * jnp.exp(-cum)).astype(BF16)
            ke_b = (kf * jnp.exp(last - cum)).astype(BF16)
            v = v_ref[rs, vs]
            intra = lax.dot_general(q_b, k_b, (((1,), (1,)), ((), ())), preferred_element_type=F32)
            intra = jnp.where(colc <= rowc, intra, 0.0)
            state = st_ref[h]
            o = (jnp.dot(intra.astype(BF16), v, preferred_element_type=F32)
                 + lax.dot_general(q_b, state.astype(BF16), (((1,), (1,)), ((), ())),
                                   preferred_element_type=F32))
            st_ref[h] = state * jnp.exp(last) + lax.dot_general(
                v, ke_b, (((0,), (0,)), ((), ())), preferred_element_type=F32)
            ms = jnp.mean(o * o, axis=-1, keepdims=True)
            o = o * lax.rsqrt(ms + EPS) * og_ref[...]
            rg = r_ref[rs, vs].astype(F32)
            y_ref[rs, vs] = (o * (rg * _sigmoid(rg))).astype(y_ref.dtype)


def _gla(proj, gx, gate_w2, gate_b, out_gain, batch, seq, heads):
    t = proj.shape[0]
    rank, d_k = gate_w2.shape
    d_v = heads * out_gain.shape[0]
    nt = seq // SEQ_TILE
    kb = d_k // d_k
    w2 = jnp.zeros((LANES, d_k), F32).at[:rank].set(gate_w2).astype(BF16)
    tok = lambda b, j: (b * nt + j, 0)
    vblk = 2 * d_k // d_v
    return pl.pallas_call(
        functools.partial(_gla_kernel, heads=heads, scale=(d_k // heads) ** -0.5),
        grid=(batch, nt),
        in_specs=[pl.BlockSpec((SEQ_TILE, d_k), lambda b, j: (b * nt + j, 0)),
                  pl.BlockSpec((SEQ_TILE, d_k), lambda b, j: (b * nt + j, kb)),
                  pl.BlockSpec((SEQ_TILE, d_v), lambda b, j: (b * nt + j, vblk)),
                  pl.BlockSpec((SEQ_TILE, d_v), lambda b, j: (b * nt + j, vblk + 1)),
                  pl.BlockSpec((SEQ_TILE, LANES), tok),
                  pl.BlockSpec((LANES, d_k), lambda b, j: (0, 0)),
                  pl.BlockSpec((1, d_k), lambda b, j: (0, 0)),
                  pl.BlockSpec((1, d_v // heads), lambda b, j: (0, 0))],
        out_specs=pl.BlockSpec((SEQ_TILE, d_v), tok),
        out_shape=jax.ShapeDtypeStruct((t, d_v), BF16),
        scratch_shapes=[pltpu.VMEM((heads, d_v // heads, d_k // heads), F32)],
        compiler_params=_params("parallel", "arbitrary"),
        name="gla",
    )(proj, proj, proj, proj, gx, w2, gate_b[None, :], out_gain[None, :])


def _route_kernel(x_ref, gain_ref, sc_ref, sh_ref, rw_ref, rb_ref, tri_ref,
                  h_ref, eidx_ref, gw_ref, rank_ref, cnt_ref, *, n_groups):
    n_exp = rw_ref.shape[0]
    epg = n_exp // n_groups
    tm = x_ref.shape[0]

    @pl.when(pl.program_id(0) == 0)
    def _():
        cnt_ref[...] = jnp.zeros_like(cnt_ref)

    h = _rms_mod(x_ref[...], gain_ref[...], sc_ref[0], sh_ref[0])
    h_ref[...] = h.astype(h_ref.dtype)
    logits = lax.dot_general(rw_ref[...], h, (((1,), (1,)), ((), ())),
                             preferred_element_type=F32, precision=HIGHEST)
    scores = _sigmoid(logits)
    sel = scores + rb_ref[...]

    sub = lax.broadcasted_iota(I32, (epg, tm), 0)

    def top2(v):
        m1 = jnp.max(v, axis=0, keepdims=True)
        i1 = jnp.min(jnp.where(v == m1, sub, epg), axis=0, keepdims=True)
        v2 = jnp.where(sub == i1, -jnp.inf, v)
        m2 = jnp.max(v2, axis=0, keepdims=True)
        i2 = jnp.min(jnp.where(v2 == m2, sub, epg), axis=0, keepdims=True)
        return m1, i1, m2, i2

    groups = [sel[g * epg:(g + 1) * epg, :] for g in range(n_groups)]
    best = None
    for g in range(n_groups):
        m1, _, m2, _ = top2(groups[g])
        gs = m1 + m2
        if g == 0:
            best, g_idx = gs, jnp.zeros((1, tm), I32)
        else:
            better = gs > best
            g_idx = jnp.where(better, g, g_idx)
            best = jnp.where(better, gs, best)
    in_group = groups[0]
    sc_group = scores[0:epg, :]
    for g in range(1, n_groups):
        in_group = jnp.where(g_idx == g, groups[g], in_group)
        sc_group = jnp.where(g_idx == g, scores[g * epg:(g + 1) * epg, :], sc_group)
    _, i1, _, i2 = top2(in_group)
    w0 = jnp.sum(jnp.where(sub == i1, sc_group, 0.0), axis=0, keepdims=True)
    w1 = jnp.sum(jnp.where(sub == i2, sc_group, 0.0), axis=0, keepdims=True)
    wsum = w0 + w1
    e0 = g_idx * epg + i1
    e1 = g_idx * epg + i2
    eidx_ref[0:1, :] = e0
    eidx_ref[1:2, :] = e1
    gw_ref[0:1, :] = w0 / wsum
    gw_ref[1:2, :] = w1 / wsum

    erow = lax.broadcasted_iota(I32, (n_exp, tm), 0)
    oh0 = erow == e0
    oh1 = erow == e1
    taken = jnp.where(oh0 | oh1, 1.0, 0.0)
    before = jnp.dot(taken.astype(BF16), tri_ref[...], preferred_element_type=F32) + cnt_ref[:, 0:1]
    r0 = jnp.sum(jnp.where(oh0, before, 0.0), axis=0, keepdims=True)
    r1 = jnp.sum(jnp.where(oh1, before, 0.0), axis=0, keepdims=True)
    rank_ref[0:1, :] = r0.astype(I32)
    rank_ref[1:2, :] = r1.astype(I32)
    cnt_ref[...] = cnt_ref[...] + jnp.sum(taken, axis=1, keepdims=True)


def _route(x2, gain, sc, sh, router_w, router_b, seq):
    t, d = x2.shape
    n_exp = router_w.shape[1]
    tps = seq // TOKEN_TILE
    pos = jnp.arange(TOKEN_TILE)
    tri = (pos[:, None] < pos[None, :]).astype(BF16)
    tokl = lambda i: (0, i)
    return pl.pallas_call(
        functools.partial(_route_kernel, n_groups=N_GROUPS),
        grid=(t // TOKEN_TILE,),
        in_specs=[pl.BlockSpec((TOKEN_TILE, d), lambda i: (i, 0)),
                  pl.BlockSpec((1, d), lambda i: (0, 0)),
                  pl.BlockSpec((1, 1, d), lambda i: (i // tps, 0, 0)),
                  pl.BlockSpec((1, 1, d), lambda i: (i // tps, 0, 0)),
                  pl.BlockSpec((n_exp, d), lambda i: (0, 0)),
                  pl.BlockSpec((n_exp, 1), lambda i: (0, 0)),
                  pl.BlockSpec((TOKEN_TILE, TOKEN_TILE), lambda i: (0, 0))],
        out_specs=[pl.BlockSpec((TOKEN_TILE, d), lambda i: (i, 0)),
                   pl.BlockSpec((TOP_K, TOKEN_TILE), tokl),
                   pl.BlockSpec((TOP_K, TOKEN_TILE), tokl),
                   pl.BlockSpec((TOP_K, TOKEN_TILE), tokl),
                   pl.BlockSpec((n_exp, LANES), lambda i: (0, 0))],
        out_shape=[jax.ShapeDtypeStruct((t, d), F32),
                   jax.ShapeDtypeStruct((TOP_K, t), I32),
                   jax.ShapeDtypeStruct((TOP_K, t), F32),
                   jax.ShapeDtypeStruct((TOP_K, t), I32),
                   jax.ShapeDtypeStruct((n_exp, LANES), F32)],
        compiler_params=_params("arbitrary"),
        name="moe_route",
    )(x2, gain, sc, sh, router_w.T, router_b[:, None], tri)


def _row_copy(table_hbm, row, buf, slot, r, sem):
    return pltpu.make_async_copy(table_hbm.at[pl.ds(row, 1), :], buf.at[slot, pl.ds(r, 1), :], sem.at[slot])


def _expert_kernel(blk_e_ref, nact_ref, src_ref, h_hbm, w1_ref, w3_ref, w2_ref, o_ref,
                   xbuf, sem, w1b, w3b, w2b):
    b = pl.program_id(0)
    n_act = nact_ref[0]
    rows = xbuf.shape[1]

    def start_gather(blk, slot):
        def issue(r, carry):
            _row_copy(h_hbm, src_ref[blk * rows + r], xbuf, slot, r, sem).start()
            return carry
        lax.fori_loop(0, rows, issue, 0, unroll=DMA_UNROLL)

    @pl.when(b == 0)
    def _():
        start_gather(0, 0)

    @pl.when(b + 1 < n_act)
    def _():
        start_gather(b + 1, (b + 1) % 2)

    @pl.when(b < n_act)
    def _():
        slot = b % 2

        def wait_row(r, carry):
            _row_copy(h_hbm, 0, xbuf, slot, r, sem).wait()
            return carry
        lax.fori_loop(0, rows, wait_row, 0, unroll=DMA_UNROLL)

        changed = blk_e_ref[b] != blk_e_ref[jnp.maximum(b - 1, 0)]

        @pl.when((b == 0) | changed)
        def _():
            w1b[...] = w1_ref[0, 0].astype(BF16)
            w3b[...] = w3_ref[0, 0].astype(BF16)
            w2b[...] = w2_ref[0, 0].astype(BF16)

        xb = xbuf[slot].astype(BF16)
        a = jnp.dot(xb, w1b[...], preferred_element_type=F32)
        hid = (a * _sigmoid(a)) * jnp.dot(xb, w3b[...], preferred_element_type=F32)
        o_ref[...] = jnp.dot(hid.astype(BF16), w2b[...], preferred_element_type=F32)

    @pl.when(b >= n_act)
    def _():
        o_ref[...] = jnp.zeros_like(o_ref)


def _experts(h, src, blk_e, n_active, w1, w3, w2, layer):
    d = h.shape[1]
    p_len = src.shape[0]
    de = w1.shape[-1]
    n_blocks = p_len // EXPERT_ROWS

    def last_active(b, nact_ref):
        return jnp.maximum(jnp.minimum(b, nact_ref[0] - 1), 0)

    def row_map(b, blk_e_ref, nact_ref, src_ref):
        return (b, 0)

    def w_map(b, blk_e_ref, nact_ref, src_ref):
        return (layer, blk_e_ref[last_active(b, nact_ref)], 0, 0)

    return pl.pallas_call(
        _expert_kernel,
        grid_spec=pltpu.PrefetchScalarGridSpec(
            num_scalar_prefetch=3,
            grid=(n_blocks,),
            in_specs=[pl.BlockSpec(memory_space=pl.ANY),
                      pl.BlockSpec((1, 1, d, de), w_map),
                      pl.BlockSpec((1, 1, d, de), w_map),
                      pl.BlockSpec((1, 1, de, d), w_map)],
            out_specs=pl.BlockSpec((EXPERT_ROWS, d), row_map),
            scratch_shapes=[pltpu.VMEM((2, EXPERT_ROWS, d), F32), pltpu.SemaphoreType.DMA((2,)),
                            pltpu.VMEM((d, de), BF16), pltpu.VMEM((d, de), BF16), pltpu.VMEM((de, d), BF16)]),
        out_shape=jax.ShapeDtypeStruct((p_len, d), F32),
        compiler_params=_params("arbitrary"),
        name="moe_experts",
    )(blk_e, n_active, src, h, w1, w3, w2)


def _combine_kernel(dest_ref, x_ref, g_ref, gw_ref, y_hbm, o_ref, ybuf, sem):
    i = pl.program_id(0)
    n_tiles = pl.num_programs(0)
    tm = x_ref.shape[0]
    n_tok = n_tiles * tm

    def start_gather(tile, slot):
        for k in range(TOP_K):
            def issue(r, carry):
                _row_copy(y_hbm, dest_ref[k * n_tok + tile * tm + r], ybuf, slot, k * tm + r, sem).start()
                return carry
            lax.fori_loop(0, tm, issue, 0, unroll=DMA_UNROLL)

    @pl.when(i == 0)
    def _():
        start_gather(0, 0)

    @pl.when(i + 1 < n_tiles)
    def _():
        start_gather(i + 1, (i + 1) % 2)

    slot = i % 2

    def wait_row(r, carry):
        _row_copy(y_hbm, 0, ybuf, slot, r, sem).wait()
        return carry
    lax.fori_loop(0, TOP_K * tm, wait_row, 0, unroll=DMA_UNROLL)

    moe = ybuf[slot, 0:tm, :] * gw_ref[:, 0:1]
    for k in range(1, TOP_K):
        moe = moe + ybuf[slot, k * tm:(k + 1) * tm, :] * gw_ref[:, k:k + 1]
    o_ref[...] = x_ref[...] + g_ref[0] * moe


def _combine(x2, g, y_pad, dest, gw_t, seq):
    t, d = x2.shape
    tps = seq // TOKEN_TILE
    tok = lambda i, dest_ref: (i, 0)
    return pl.pallas_call(
        _combine_kernel,
        grid_spec=pltpu.PrefetchScalarGridSpec(
            num_scalar_prefetch=1,
            grid=(t // TOKEN_TILE,),
            in_specs=[pl.BlockSpec((TOKEN_TILE, d), tok),
                      pl.BlockSpec((1, 1, d), lambda i, dest_ref: (i // tps, 0, 0)),
                      pl.BlockSpec((TOKEN_TILE, TOP_K), tok),
                      pl.BlockSpec(memory_space=pl.ANY)],
            out_specs=pl.BlockSpec((TOKEN_TILE, d), tok),
            scratch_shapes=[pltpu.VMEM((2, TOP_K * TOKEN_TILE, d), F32), pltpu.SemaphoreType.DMA((2,))]),
        out_shape=jax.ShapeDtypeStruct((t, d), F32),
        compiler_params=_params("arbitrary"),
        name="moe_combine",
    )(dest.reshape(-1), x2, g, gw_t, y_pad)


def _moe(x2, gain, sc, sh, g, router_w, router_b, w1, w3, w2, layer, seq):
    t, d = x2.shape
    n_exp = router_w.shape[1]
    h, e_idx, gw, rank, cnt = _route(x2, gain, sc, sh, router_w, router_b, seq)
    counts = cnt[:, 0].astype(I32)
    padded = (counts + EXPERT_ROWS - 1) // EXPERT_ROWS * EXPERT_ROWS
    pend = jnp.cumsum(padded)
    pstart = pend - padded
    expert_ids = jnp.arange(n_exp, dtype=I32)[:, None, None]
    dest = jnp.sum(jnp.where(e_idx[None] == expert_ids, pstart[:, None, None], 0), axis=0) + rank
    n_blocks = -(-(t * TOP_K) // EXPERT_ROWS) + n_exp
    p_len = n_blocks * EXPERT_ROWS
    blk_start = jnp.arange(n_blocks, dtype=I32) * EXPERT_ROWS
    blk_e = jnp.minimum(jnp.sum((pend[None, :] <= blk_start[:, None]).astype(I32), axis=1), n_exp - 1)
    n_active = (pend[-1:] // EXPERT_ROWS).astype(I32)
    tok_ids = jnp.broadcast_to(jnp.arange(t, dtype=I32)[None, :], dest.shape)
    src = jnp.zeros((p_len,), I32).at[dest.reshape(-1)].set(tok_ids.reshape(-1), unique_indices=True)
    y_pad = _experts(h, src, blk_e, n_active, w1, w3, w2, layer)
    return _combine(x2, g, y_pad, dest, gw.T, seq)


def _block_diag(w):
    n, a, b = w.shape
    eye = jnp.eye(n, dtype=w.dtype)
    return (eye[:, None, :, None] * w[:, :, None, :]).reshape(n * a, n * b)


def kernel(x, c, router_w, router_b, ada_w, ada_b, norm_mix, norm_ffn, ev_w_in, ev_conv_w, ev_conv_b, ev_r_w, ev_r_b, ev_i_w, ev_i_b, ev_lam, ev_forget_b, ev_q_gain, ev_k_gain, ev_w_out, od_w_in, od_gate_w2, od_gate_b, od_out_gain, od_w_out, moe_w1, moe_w3, moe_w2):
    batch, seq, d = x.shape
    depth = ada_w.shape[0]
    x2 = x.reshape(batch * seq, d)
    mods = _ada(c, ada_w, ada_b)

    for layer in range(depth):
        sh_m, sc_m, g_m, sh_f, sc_f, g_f = [mods[layer, :, None, k * d:(k + 1) * d] for k in range(6)]
        j = layer // 2
        if layer % 2 == 0:
            d_rnn = ev_conv_w.shape[-1]
            heads, head_dim = ev_forget_b.shape[-1], ev_q_gain.shape[-1]
            d_fox = heads * head_dim
            n_main = 2 * d_rnn + 3 * d_fox
            w_in = ev_w_in[j]
            wx = jnp.zeros((d, LANES), F32).at[:, :heads].set(w_in[:, n_main:]).astype(BF16)
            proj, fx = _in_proj(x2, norm_mix[layer][None, :], sc_m, sh_m, w_in[:, :n_main].astype(BF16), wx, seq)
            wg = jnp.concatenate([_block_diag(ev_r_w[j]), _block_diag(ev_i_w[j])], axis=1).astype(BF16)
            bg = jnp.concatenate([ev_r_b[j].reshape(-1), ev_i_b[j].reshape(-1)])[None, :]
            y_rnn = _rglru(proj, ev_conv_w[j], ev_conv_b[j][None, :], wg, bg, ev_lam[j][None, :], batch, seq)
            qt, kn, vt, f = _fox_prep(proj, fx, ev_forget_b[j], ev_q_gain[j], ev_k_gain[j],
                                      batch, seq, heads, head_dim)
            y_fox = _fox_attn(qt, kn, vt, f, batch, seq, heads, head_dim)
            w_out = ev_w_out[j].astype(BF16)
            x2 = _out_proj([y_rnn, y_fox], [w_out[:d_rnn], w_out[d_rnn:]], x2, g_m, seq)
        else:
            rank, d_k = od_gate_w2.shape[1:]
            d_v = od_w_out.shape[1]
            heads = d_v // od_out_gain.shape[-1]
            n_main = 2 * d_k + 2 * d_v
            w_in = od_w_in[j]
            wx = jnp.zeros((d, LANES), F32).at[:, :rank].set(w_in[:, n_main:]).astype(BF16)
            proj, gx = _in_proj(x2, norm_mix[layer][None, :], sc_m, sh_m, w_in[:, :n_main].astype(BF16), wx, seq)
            y = _gla(proj, gx, od_gate_w2[j], od_gate_b[j], od_out_gain[j], batch, seq, heads)
            x2 = _out_proj([y], [od_w_out[j].astype(BF16)], x2, g_m, seq)
        x2 = _moe(x2, norm_ffn[layer][None, :], sc_f, sh_f, g_f, router_w, router_b,
                  moe_w1, moe_w3, moe_w2, layer, seq)
    return x2.reshape(batch, seq, d)
```

```python
import functools

import jax
import jax.numpy as jnp
from jax import lax
from jax.experimental import pallas as pl
from jax.experimental.pallas import tpu as pltpu

F32 = jnp.float32
BF16 = jnp.bfloat16
I32 = jnp.int32
HIGHEST = lax.Precision.HIGHEST

EPS = 1e-6
RGLRU_C = 8.0
CONV_W = 4
GATE_TAU = 16.0
GLA_CHUNK = 64
N_GROUPS = 4
TOP_K = 2
NEG_BIG = -1e30
LOG2E = 1.4426950408889634

LANES = 128
SUBLANES = 8
TOKEN_TILE = 512
SEQ_TILE = 256
ATTN_TILE = 256
EXPERT_ROWS = 256
DMA_UNROLL = 16
ROW_CHUNKS = 8
VMEM_LIMIT = 48 * 1024 * 1024


def _params(*sem):
    return pltpu.CompilerParams(dimension_semantics=sem, vmem_limit_bytes=VMEM_LIMIT)


def _rms_mod(x, gain, sc, sh):
    ms = jnp.mean(x * x, axis=-1, keepdims=True)
    return (x * lax.rsqrt(ms + EPS) * gain) * (1.0 + sc) + sh


def _softplus(z):
    return jnp.maximum(z, 0.0) + jnp.log1p(jnp.exp(-jnp.abs(z)))


def _sigmoid(z):
    return 1.0 / (1.0 + jnp.exp(-z))


def _shift_rows(v, d, fill, row):
    return jnp.where(row >= d, pltpu.roll(v, d, axis=0), fill)


def _cumsum_rows(v):
    n = v.shape[0]
    row = lax.broadcasted_iota(I32, v.shape, 0)
    d = 1
    while d < n:
        v = v + _shift_rows(v, d, 0.0, row)
        d *= 2
    return v


def _ada_kernel(c_ref, w_ref, b_ref, o_ref):
    c = c_ref[...]
    o_ref[0] = jnp.dot(c * _sigmoid(c), w_ref[0], preferred_element_type=F32, precision=HIGHEST) + b_ref[0]


def _ada(c, ada_w, ada_b):
    depth, d, n = ada_w.shape
    b = c.shape[0]
    tn = n // 4
    return pl.pallas_call(
        _ada_kernel,
        grid=(depth, n // tn),
        in_specs=[pl.BlockSpec((b, d), lambda l, j: (0, 0)),
                  pl.BlockSpec((1, d, tn), lambda l, j: (l, 0, j)),
                  pl.BlockSpec((1, 1, tn), lambda l, j: (l, 0, j))],
        out_specs=pl.BlockSpec((1, b, tn), lambda l, j: (l, 0, j)),
        out_shape=jax.ShapeDtypeStruct((depth, b, n), F32),
        compiler_params=_params("parallel", "parallel"),
        name="ada_mod",
    )(c, ada_w, ada_b.reshape(depth, 1, n))


def _in_proj_kernel(x_ref, gain_ref, sc_ref, sh_ref, w_ref, wx_ref, o_ref, ox_ref, *, n_chunk):
    h = _rms_mod(x_ref[...], gain_ref[...], sc_ref[0], sh_ref[0]).astype(BF16)
    for c in range(0, o_ref.shape[1], n_chunk):
        o_ref[:, c:c + n_chunk] = jnp.dot(h, w_ref[:, c:c + n_chunk],
                                          preferred_element_type=F32).astype(o_ref.dtype)
    ox_ref[...] = jnp.dot(h, wx_ref[...], preferred_element_type=F32)


def _in_proj(x2, gain, sc, sh, w, wx, seq):
    t, d = x2.shape
    n = w.shape[1]
    tps = seq // TOKEN_TILE
    return pl.pallas_call(
        functools.partial(_in_proj_kernel, n_chunk=512),
        grid=(t // TOKEN_TILE,),
        in_specs=[pl.BlockSpec((TOKEN_TILE, d), lambda i: (i, 0)),
                  pl.BlockSpec((1, d), lambda i: (0, 0)),
                  pl.BlockSpec((1, 1, d), lambda i: (i // tps, 0, 0)),
                  pl.BlockSpec((1, 1, d), lambda i: (i // tps, 0, 0)),
                  pl.BlockSpec((d, n), lambda i: (0, 0)),
                  pl.BlockSpec((d, LANES), lambda i: (0, 0))],
        out_specs=[pl.BlockSpec((TOKEN_TILE, n), lambda i: (i, 0)),
                   pl.BlockSpec((TOKEN_TILE, LANES), lambda i: (i, 0))],
        out_shape=[jax.ShapeDtypeStruct((t, n), BF16), jax.ShapeDtypeStruct((t, LANES), F32)],
        compiler_params=_params("parallel"),
        name="in_proj",
    )(x2, gain, sc, sh, w, wx)


def _out_proj_kernel(*refs, n_parts):
    y_refs, w_refs = refs[:n_parts], refs[n_parts:2 * n_parts]
    x_ref, g_ref, o_ref = refs[2 * n_parts:]
    acc = jnp.dot(y_refs[0][...], w_refs[0][...], preferred_element_type=F32)
    for y_ref, w_ref in zip(y_refs[1:], w_refs[1:]):
        acc = acc + jnp.dot(y_ref[...], w_ref[...], preferred_element_type=F32)
    o_ref[...] = x_ref[...] + g_ref[0] * acc


def _out_proj(ys, ws, x2, g, seq):
    t, d = x2.shape
    tps = seq // TOKEN_TILE
    n_parts = len(ys)
    in_specs = [pl.BlockSpec((TOKEN_TILE, y.shape[1]), lambda i: (i, 0)) for y in ys]
    in_specs += [pl.BlockSpec(w.shape, lambda i: (0, 0)) for w in ws]
    in_specs += [pl.BlockSpec((TOKEN_TILE, d), lambda i: (i, 0)),
                 pl.BlockSpec((1, 1, d), lambda i: (i // tps, 0, 0))]
    return pl.pallas_call(
        functools.partial(_out_proj_kernel, n_parts=n_parts),
        grid=(t // TOKEN_TILE,),
        in_specs=in_specs,
        out_specs=pl.BlockSpec((TOKEN_TILE, d), lambda i: (i, 0)),
        out_shape=jax.ShapeDtypeStruct((t, d), F32),
        compiler_params=_params("parallel"),
        name="out_proj",
    )(*ys, *ws, x2, g)


def _rglru_kernel(x_ref, gate_ref, cw_ref, cb_ref, wg_ref, bg_ref, lam_ref, y_ref, ext_ref, hc_ref):
    rows, ch = x_ref.shape

    @pl.when(pl.program_id(1) == 0)
    def _():
        ext_ref[0:SUBLANES, :] = jnp.zeros((SUBLANES, ch), F32)
        hc_ref[...] = jnp.zeros_like(hc_ref)

    x = x_ref[...].astype(F32)
    ext_ref[SUBLANES:SUBLANES + rows, :] = x
    xc = cb_ref[...] + cw_ref[CONV_W - 1:CONV_W, :] * x
    for j in range(CONV_W - 1):
        off = SUBLANES - (CONV_W - 1) + j
        xc = xc + cw_ref[j:j + 1, :] * ext_ref[off:off + rows, :]
    ext_ref[0:SUBLANES, :] = x[rows - SUBLANES:, :]

    gates = jnp.dot(xc.astype(BF16), wg_ref[...], preferred_element_type=F32) + bg_ref[...]
    r = _sigmoid(gates[:, :ch])
    gi = _sigmoid(gates[:, ch:])
    log_a = (-RGLRU_C * _softplus(-lam_ref[...])) * r
    a = jnp.exp(log_a)
    u = jnp.sqrt(1.0 - a * a) * (gi * xc)

    row = lax.broadcasted_iota(I32, (rows, ch), 0)
    d = 1
    while d < rows:
        u = u + a * _shift_rows(u, d, 0.0, row)
        a = a * _shift_rows(a, d, 1.0, row)
        d *= 2
    h = u + a * hc_ref[...]
    hc_ref[...] = h[rows - 1:rows, :]
    y_ref[...] = (h * jax.nn.gelu(gate_ref[...].astype(F32))).astype(y_ref.dtype)


def _rglru(proj, conv_w, conv_b, wg, bg, lam, batch, seq):
    t = proj.shape[0]
    ch = conv_w.shape[1]
    nt = seq // SEQ_TILE
    return pl.pallas_call(
        _rglru_kernel,
        grid=(batch, nt),
        in_specs=[pl.BlockSpec((SEQ_TILE, ch), lambda b, j: (b * nt + j, 0)),
                  pl.BlockSpec((SEQ_TILE, ch), lambda b, j: (b * nt + j, 1)),
                  pl.BlockSpec((CONV_W, ch), lambda b, j: (0, 0)),
                  pl.BlockSpec((1, ch), lambda b, j: (0, 0)),
                  pl.BlockSpec((ch, 2 * ch), lambda b, j: (0, 0)),
                  pl.BlockSpec((1, 2 * ch), lambda b, j: (0, 0)),
                  pl.BlockSpec((1, ch), lambda b, j: (0, 0))],
        out_specs=pl.BlockSpec((SEQ_TILE, ch), lambda b, j: (b * nt + j, 0)),
        out_shape=jax.ShapeDtypeStruct((t, ch), BF16),
        scratch_shapes=[pltpu.VMEM((SEQ_TILE + SUBLANES, ch), F32), pltpu.VMEM((1, ch), F32)],
        compiler_params=_params("parallel", "arbitrary"),
        name="rglru",
    )(proj, proj, conv_w, conv_b, wg, bg, lam)


def _fox_prep_kernel(q_ref, k_ref, v_ref, fx_ref, fb_ref, qg_ref, kg_ref, ones_ref, qt_ref, kn_ref, vt_ref, f_ref,
                     fc_ref, *, scale):
    @pl.when(pl.program_id(1) == 0)
    def _():
        fc_ref[...] = jnp.zeros_like(fc_ref)

    def qk_norm(v_ref, g_ref, mult):
        v = v_ref[...].astype(F32)
        ms = jnp.dot((v * v).astype(BF16), ones_ref[...], preferred_element_type=F32)
        return (v * lax.rsqrt(ms + EPS) * g_ref[...]) * mult

    qt_ref[0, 0] = qk_norm(q_ref, qg_ref, scale * LOG2E).T.astype(qt_ref.dtype)
    kn_ref[...] = qk_norm(k_ref, kg_ref, 1.0).astype(kn_ref.dtype)
    vt_ref[0, 0] = v_ref[...].astype(F32).T.astype(vt_ref.dtype)

    log_f = -_softplus(-(fx_ref[...] + fb_ref[...])) * LOG2E
    f = _cumsum_rows(log_f) + fc_ref[...]
    fc_ref[...] = f[f.shape[0] - 1:, :]
    f_ref[...] = f


def _fox_prep(proj, fx, forget_b, q_gain, k_gain, batch, seq, heads, head_dim):
    t = proj.shape[0]
    dq = heads * head_dim
    nt = seq // ATTN_TILE
    qblk = (proj.shape[1] - 3 * dq) // dq
    fb = jnp.zeros((1, LANES), F32).at[0, :heads].set(forget_b)
    qg = jnp.tile(q_gain, heads)[None, :]
    kg = jnp.tile(k_gain, heads)[None, :]
    head_of = jnp.arange(dq) // head_dim
    ones = (head_of[:, None] == head_of[None, :]).astype(BF16) * (1.0 / head_dim)
    tok = lambda b, j: (b * nt + j, 0)
    return pl.pallas_call(
        functools.partial(_fox_prep_kernel, scale=head_dim ** -0.5),
        grid=(batch, nt),
        in_specs=[pl.BlockSpec((ATTN_TILE, dq), lambda b, j: (b * nt + j, qblk)),
                  pl.BlockSpec((ATTN_TILE, dq), lambda b, j: (b * nt + j, qblk + 1)),
                  pl.BlockSpec((ATTN_TILE, dq), lambda b, j: (b * nt + j, qblk + 2)),
                  pl.BlockSpec((ATTN_TILE, LANES), tok),
                  pl.BlockSpec((1, LANES), lambda b, j: (0, 0)),
                  pl.BlockSpec((1, dq), lambda b, j: (0, 0)),
                  pl.BlockSpec((1, dq), lambda b, j: (0, 0)),
                  pl.BlockSpec((dq, dq), lambda b, j: (0, 0))],
        out_specs=[pl.BlockSpec((1, 1, dq, ATTN_TILE), lambda b, j: (b, j, 0, 0)),
                   pl.BlockSpec((ATTN_TILE, dq), tok),
                   pl.BlockSpec((1, 1, dq, ATTN_TILE), lambda b, j: (b, j, 0, 0)),
                   pl.BlockSpec((ATTN_TILE, LANES), tok)],
        out_shape=[jax.ShapeDtypeStruct((batch, nt, dq, ATTN_TILE), BF16),
                   jax.ShapeDtypeStruct((t, dq), BF16),
                   jax.ShapeDtypeStruct((batch, nt, dq, ATTN_TILE), BF16),
                   jax.ShapeDtypeStruct((t, LANES), F32)],
        scratch_shapes=[pltpu.VMEM((1, LANES), F32)],
        compiler_params=_params("parallel", "arbitrary"),
        name="fox_prep",
    )(proj, proj, proj, fx, fb, qg, kg, ones)


def _fox_attn_kernel(qt_ref, k_ref, vt_ref, f_ref, y_ref, qm_ref, lg_ref, m_ref, l_ref, acc_ref, *,
                     heads, head_dim):
    tq = y_ref.shape[0]
    i = pl.program_id(1)
    pair = LANES // head_dim
    feat = lax.broadcasted_iota(I32, (LANES, tq), 0)
    key_i = lax.broadcasted_iota(I32, (tq, tq), 0)
    qry_i = lax.broadcasted_iota(I32, (tq, tq), 1)

    for h in range(heads):
        p, s = divmod(h, pair)
        q2t = qt_ref[0, 0, p * LANES:(p + 1) * LANES, :]
        qm_ref[h] = jnp.where((feat >= s * head_dim) & (feat < (s + 1) * head_dim), q2t, jnp.zeros_like(q2t))
    m_ref[...] = jnp.full(m_ref.shape, NEG_BIG, F32)
    l_ref[...] = jnp.zeros(l_ref.shape, F32)
    acc_ref[...] = jnp.zeros(acc_ref.shape, F32)

    def tile(j, masked):
        start = pl.multiple_of(j * tq, tq)
        fk = f_ref[pl.ds(start, tq), :]
        for h in range(heads):
            p = h // pair
            k2 = k_ref[pl.ds(start, tq), p * LANES:(p + 1) * LANES]
            lg_ref[h] = jnp.dot(k2, qm_ref[h], preferred_element_type=F32)
        for h in range(heads):
            vt = vt_ref[0, j, h * head_dim:(h + 1) * head_dim, :]
            logit = lg_ref[h] - fk[:, h:h + 1]
            if masked:
                logit = jnp.where(key_i <= qry_i, logit, NEG_BIG)
            m_old = m_ref[h][0:1, :]
            m_new = jnp.maximum(m_old, jnp.max(logit, axis=0, keepdims=True))
            alpha = jnp.exp2(m_old - m_new)
            pr = jnp.exp2(logit - m_new)
            l_new = alpha * l_ref[h][0:1, :] + jnp.sum(pr, axis=0, keepdims=True)
            acc_ref[h] = alpha * acc_ref[h] + jnp.dot(vt, pr.astype(BF16), preferred_element_type=F32)
            m_ref[h] = jnp.broadcast_to(m_new, (SUBLANES, tq))
            l_ref[h] = jnp.broadcast_to(l_new, (SUBLANES, tq))

    def body(j, carry):
        tile(j, False)
        return carry

    lax.fori_loop(0, i, body, 0)
    tile(i, True)

    for p in range(heads // pair):
        out_t = jnp.concatenate([acc_ref[h] / l_ref[h][0:1, :] for h in range(p * pair, (p + 1) * pair)], axis=0)
        y_ref[:, p * LANES:(p + 1) * LANES] = out_t.T.astype(y_ref.dtype)


def _fox_attn(qt, kn, vt, f, batch, seq, heads, head_dim):
    t, dq = kn.shape
    nq = seq // ATTN_TILE
    return pl.pallas_call(
        functools.partial(_fox_attn_kernel, heads=heads, head_dim=head_dim),
        grid=(batch, nq),
        in_specs=[pl.BlockSpec((1, 1, dq, ATTN_TILE), lambda b, i: (b, i, 0, 0)),
                  pl.BlockSpec((seq, dq), lambda b, i: (b, 0)),
                  pl.BlockSpec((1, nq, dq, ATTN_TILE), lambda b, i: (b, 0, 0, 0)),
                  pl.BlockSpec((seq, LANES), lambda b, i: (b, 0))],
        out_specs=pl.BlockSpec((ATTN_TILE, dq), lambda b, i: (b * nq + i, 0)),
        out_shape=jax.ShapeDtypeStruct((t, dq), BF16),
        scratch_shapes=[pltpu.VMEM((heads, LANES, ATTN_TILE), BF16),
                        pltpu.VMEM((heads, ATTN_TILE, ATTN_TILE), F32),
                        pltpu.VMEM((heads, SUBLANES, ATTN_TILE), F32),
                        pltpu.VMEM((heads, SUBLANES, ATTN_TILE), F32),
                        pltpu.VMEM((heads, head_dim, ATTN_TILE), F32)],
        compiler_params=_params("parallel", "arbitrary"),
        name="fox_attn",
    )(qt, kn, vt, f)


def _gla_kernel(q_ref, k_ref, v_ref, r_ref, gx_ref, w2_ref, gb_ref, og_ref, y_ref, st_ref, *, heads, scale):
    rows = q_ref.shape[0]
    dk = q_ref.shape[1] // heads
    dv = v_ref.shape[1] // heads
    ck = GLA_CHUNK

    @pl.when(pl.program_id(1) == 0)
    def _():
        st_ref[...] = jnp.zeros_like(st_ref)

    g = jnp.dot(gx_ref[...].astype(BF16), w2_ref[...], preferred_element_type=F32) + gb_ref[...]
    log_g = -_softplus(-g) * (1.0 / GATE_TAU)
    rowc = lax.broadcasted_iota(I32, (ck, ck), 0)
    colc = lax.broadcasted_iota(I32, (ck, ck), 1)

    for c in range(rows // ck):
        rs = slice(c * ck, (c + 1) * ck)
        for h in range(heads):
            ks = slice(h * dk, (h + 1) * dk)
            vs = slice(h * dv, (h + 1) * dv)
            cum = _cumsum_rows(log_g[rs, ks])
            last = cum[ck - 1:ck, :]
            kf = k_ref[rs, ks].astype(F32)
            q_b = (q_ref[rs, ks].astype(F32) * scale * jnp.exp(cum)).astype(BF16)
            k_b = (kf * jnp.exp(-cum)).astype(BF16)
            ke_b = (kf * jnp.exp(last - cum)).astype(BF16)
            v = v_ref[rs, vs]
            intra = lax.dot_general(q_b, k_b, (((1,), (1,)), ((), ())), preferred_element_type=F32)
            intra = jnp.where(colc <= rowc, intra, 0.0)
            state = st_ref[h]
            o = (jnp.dot(intra.astype(BF16), v, preferred_element_type=F32)
                 + lax.dot_general(q_b, state.astype(BF16), (((1,), (1,)), ((), ())),
                                   preferred_element_type=F32))
            st_ref[h] = state * jnp.exp(last) + lax.dot_general(
                v, ke_b, (((0,), (0,)), ((), ())), preferred_element_type=F32)
            ms = jnp.mean(o * o, axis=-1, keepdims=True)
            o = o * lax.rsqrt(ms + EPS) * og_ref[...]
            rg = r_ref[rs, vs].astype(F32)
            y_ref[rs, vs] = (o * (rg * _sigmoid(rg))).astype(y_ref.dtype)


def _gla(proj, gx, gate_w2, gate_b, out_gain, batch, seq, heads):
    t = proj.shape[0]
    rank, d_k = gate_w2.shape
    d_v = heads * out_gain.shape[0]
    nt = seq // SEQ_TILE
    kb = d_k // d_k
    w2 = jnp.zeros((LANES, d_k), F32).at[:rank].set(gate_w2).astype(BF16)
    tok = lambda b, j: (b * nt + j, 0)
    vblk = 2 * d_k // d_v
    return pl.pallas_call(
        functools.partial(_gla_kernel, heads=heads, scale=(d_k // heads) ** -0.5),
        grid=(batch, nt),
        in_specs=[pl.BlockSpec((SEQ_TILE, d_k), lambda b, j: (b * nt + j, 0)),
                  pl.BlockSpec((SEQ_TILE, d_k), lambda b, j: (b * nt + j, kb)),
                  pl.BlockSpec((SEQ_TILE, d_v), lambda b, j: (b * nt + j, vblk)),
                  pl.BlockSpec((SEQ_TILE, d_v), lambda b, j: (b * nt + j, vblk + 1)),
                  pl.BlockSpec((SEQ_TILE, LANES), tok),
                  pl.BlockSpec((LANES, d_k), lambda b, j: (0, 0)),
                  pl.BlockSpec((1, d_k), lambda b, j: (0, 0)),
                  pl.BlockSpec((1, d_v // heads), lambda b, j: (0, 0))],
        out_specs=pl.BlockSpec((SEQ_TILE, d_v), tok),
        out_shape=jax.ShapeDtypeStruct((t, d_v), BF16),
        scratch_shapes=[pltpu.VMEM((heads, d_v // heads, d_k // heads), F32)],
        compiler_params=_params("parallel", "arbitrary"),
        name="gla",
    )(proj, proj, proj, proj, gx, w2, gate_b[None, :], out_gain[None, :])


def _route_kernel(x_ref, gain_ref, sc_ref, sh_ref, rw_ref, rb_ref, tri_ref,
                  h_ref, eidx_ref, gw_ref, rank_ref, cnt_ref, *, n_groups):
    n_exp = rw_ref.shape[0]
    epg = n_exp // n_groups
    tm = x_ref.shape[0]

    @pl.when(pl.program_id(0) == 0)
    def _():
        cnt_ref[...] = jnp.zeros_like(cnt_ref)

    h = _rms_mod(x_ref[...], gain_ref[...], sc_ref[0], sh_ref[0])
    nch = h.shape[1] // LANES
    for s in range(nch):
        h_ref[pl.ds(s, tm, stride=nch), :] = h[:, s * LANES:(s + 1) * LANES]
    logits = lax.dot_general(rw_ref[...], h, (((1,), (1,)), ((), ())),
                             preferred_element_type=F32, precision=HIGHEST)
    scores = _sigmoid(logits)
    sel = scores + rb_ref[...]

    sub = lax.broadcasted_iota(I32, (epg, tm), 0)

    def top2(v):
        m1 = jnp.max(v, axis=0, keepdims=True)
        i1 = jnp.min(jnp.where(v == m1, sub, epg), axis=0, keepdims=True)
        v2 = jnp.where(sub == i1, -jnp.inf, v)
        m2 = jnp.max(v2, axis=0, keepdims=True)
        i2 = jnp.min(jnp.where(v2 == m2, sub, epg), axis=0, keepdims=True)
        return m1, i1, m2, i2

    groups = [sel[g * epg:(g + 1) * epg, :] for g in range(n_groups)]
    best = None
    for g in range(n_groups):
        m1, _, m2, _ = top2(groups[g])
        gs = m1 + m2
        if g == 0:
            best, g_idx = gs, jnp.zeros((1, tm), I32)
        else:
            better = gs > best
            g_idx = jnp.where(better, g, g_idx)
            best = jnp.where(better, gs, best)
    in_group = groups[0]
    sc_group = scores[0:epg, :]
    for g in range(1, n_groups):
        in_group = jnp.where(g_idx == g, groups[g], in_group)
        sc_group = jnp.where(g_idx == g, scores[g * epg:(g + 1) * epg, :], sc_group)
    _, i1, _, i2 = top2(in_group)
    w0 = jnp.sum(jnp.where(sub == i1, sc_group, 0.0), axis=0, keepdims=True)
    w1 = jnp.sum(jnp.where(sub == i2, sc_group, 0.0), axis=0, keepdims=True)
    wsum = w0 + w1
    e0 = g_idx * epg + i1
    e1 = g_idx * epg + i2
    eidx_ref[0:1, :] = e0
    eidx_ref[1:2, :] = e1
    gw_ref[0:1, :] = w0 / wsum
    gw_ref[1:2, :] = w1 / wsum

    erow = lax.broadcasted_iota(I32, (n_exp, tm), 0)
    oh0 = erow == e0
    oh1 = erow == e1
    taken = jnp.where(oh0 | oh1, 1.0, 0.0)
    before = jnp.dot(taken.astype(BF16), tri_ref[...], preferred_element_type=F32) + cnt_ref[:, 0:1]
    r0 = jnp.sum(jnp.where(oh0, before, 0.0), axis=0, keepdims=True)
    r1 = jnp.sum(jnp.where(oh1, before, 0.0), axis=0, keepdims=True)
    rank_ref[0:1, :] = r0.astype(I32)
    rank_ref[1:2, :] = r1.astype(I32)
    cnt_ref[...] = cnt_ref[...] + jnp.sum(taken, axis=1, keepdims=True)


def _route(x2, gain, sc, sh, router_w, router_b, seq):
    t, d = x2.shape
    n_exp = router_w.shape[1]
    tps = seq // TOKEN_TILE
    pos = jnp.arange(TOKEN_TILE)
    tri = (pos[:, None] < pos[None, :]).astype(BF16)
    tokl = lambda i: (0, i)
    return pl.pallas_call(
        functools.partial(_route_kernel, n_groups=N_GROUPS),
        grid=(t // TOKEN_TILE,),
        in_specs=[pl.BlockSpec((TOKEN_TILE, d), lambda i: (i, 0)),
                  pl.BlockSpec((1, d), lambda i: (0, 0)),
                  pl.BlockSpec((1, 1, d), lambda i: (i // tps, 0, 0)),
                  pl.BlockSpec((1, 1, d), lambda i: (i // tps, 0, 0)),
                  pl.BlockSpec((n_exp, d), lambda i: (0, 0)),
                  pl.BlockSpec((n_exp, 1), lambda i: (0, 0)),
                  pl.BlockSpec((TOKEN_TILE, TOKEN_TILE), lambda i: (0, 0))],
        out_specs=[pl.BlockSpec((TOKEN_TILE * (d // LANES), LANES), lambda i: (i, 0)),
                   pl.BlockSpec((TOP_K, TOKEN_TILE), tokl),
                   pl.BlockSpec((TOP_K, TOKEN_TILE), tokl),
                   pl.BlockSpec((TOP_K, TOKEN_TILE), tokl),
                   pl.BlockSpec((n_exp, LANES), lambda i: (0, 0))],
        out_shape=[jax.ShapeDtypeStruct((t * (d // LANES), LANES), F32),
                   jax.ShapeDtypeStruct((TOP_K, t), I32),
                   jax.ShapeDtypeStruct((TOP_K, t), F32),
                   jax.ShapeDtypeStruct((TOP_K, t), I32),
                   jax.ShapeDtypeStruct((n_exp, LANES), F32)],
        compiler_params=_params("arbitrary"),
        name="moe_route",
    )(x2, gain, sc, sh, router_w.T, router_b[:, None], tri)


def _row_copy(table_hbm, row, buf, slot, r, sem):
    src = table_hbm.at[pl.ds(pl.multiple_of(row * ROW_CHUNKS, ROW_CHUNKS), ROW_CHUNKS), :]
    dst = buf.at[slot, pl.ds(pl.multiple_of(r * ROW_CHUNKS, ROW_CHUNKS), ROW_CHUNKS), :]
    return pltpu.make_async_copy(src, dst, sem.at[slot])


def _expert_kernel(blk_e_ref, nact_ref, src_ref, h_hbm, w1_ref, w3_ref, w2_ref, o_ref,
                   xbuf, sem, xrow, w1b, w3b, w2b):
    b = pl.program_id(0)
    n_act = nact_ref[0]
    rows = xbuf.shape[1] // ROW_CHUNKS

    def start_gather(blk, slot):
        def issue(r, carry):
            _row_copy(h_hbm, src_ref[blk * rows + r], xbuf, slot, r, sem).start()
            return carry
        lax.fori_loop(0, rows, issue, 0, unroll=DMA_UNROLL)

    @pl.when(b == 0)
    def _():
        start_gather(0, 0)

    @pl.when(b + 1 < n_act)
    def _():
        start_gather(b + 1, (b + 1) % 2)

    @pl.when(b < n_act)
    def _():
        slot = b % 2

        def wait_row(r, carry):
            _row_copy(h_hbm, 0, xbuf, slot, r, sem).wait()
            return carry
        lax.fori_loop(0, rows, wait_row, 0, unroll=DMA_UNROLL)

        changed = blk_e_ref[b] != blk_e_ref[jnp.maximum(b - 1, 0)]

        @pl.when((b == 0) | changed)
        def _():
            w1b[...] = w1_ref[0, 0].astype(BF16)
            w3b[...] = w3_ref[0, 0].astype(BF16)
            w2b[...] = w2_ref[0, 0].astype(BF16)

        for s in range(ROW_CHUNKS):
            xrow[:, s * LANES:(s + 1) * LANES] = xbuf[slot, pl.ds(s, rows, stride=ROW_CHUNKS), :].astype(BF16)
        xb = xrow[...]
        a = jnp.dot(xb, w1b[...], preferred_element_type=F32)
        hid = (a * _sigmoid(a)) * jnp.dot(xb, w3b[...], preferred_element_type=F32)
        out = jnp.dot(hid.astype(BF16), w2b[...], preferred_element_type=F32)
        for s in range(ROW_CHUNKS):
            o_ref[pl.ds(s, rows, stride=ROW_CHUNKS), :] = out[:, s * LANES:(s + 1) * LANES]

    @pl.when(b >= n_act)
    def _():
        o_ref[...] = jnp.zeros_like(o_ref)


def _experts(h, src, blk_e, n_active, w1, w3, w2, layer):
    d = ROW_CHUNKS * LANES
    p_len = src.shape[0]
    de = w1.shape[-1]
    n_blocks = p_len // EXPERT_ROWS

    def last_active(b, nact_ref):
        return jnp.maximum(jnp.minimum(b, nact_ref[0] - 1), 0)

    def row_map(b, blk_e_ref, nact_ref, src_ref):
        return (b, 0)

    def w_map(b, blk_e_ref, nact_ref, src_ref):
        return (layer, blk_e_ref[last_active(b, nact_ref)], 0, 0)

    return pl.pallas_call(
        _expert_kernel,
        grid_spec=pltpu.PrefetchScalarGridSpec(
            num_scalar_prefetch=3,
            grid=(n_blocks,),
            in_specs=[pl.BlockSpec(memory_space=pl.ANY),
                      pl.BlockSpec((1, 1, d, de), w_map),
                      pl.BlockSpec((1, 1, d, de), w_map),
                      pl.BlockSpec((1, 1, de, d), w_map)],
            out_specs=pl.BlockSpec((EXPERT_ROWS * ROW_CHUNKS, LANES), row_map),
            scratch_shapes=[pltpu.VMEM((2, EXPERT_ROWS * ROW_CHUNKS, LANES), F32), pltpu.SemaphoreType.DMA((2,)),
                            pltpu.VMEM((EXPERT_ROWS, d), BF16), pltpu.VMEM((d, de), BF16), pltpu.VMEM((d, de), BF16), pltpu.VMEM((de, d), BF16)]),
        out_shape=jax.ShapeDtypeStruct((p_len * ROW_CHUNKS, LANES), F32),
        compiler_params=_params("arbitrary"),
        name="moe_experts",
    )(blk_e, n_active, src, h, w1, w3, w2)


def _combine_kernel(dest_ref, x_ref, g_ref, gw_ref, y_hbm, o_ref, ybuf, sem):
    i = pl.program_id(0)
    n_tiles = pl.num_programs(0)
    tm = x_ref.shape[0]
    n_tok = n_tiles * tm

    def start_gather(tile, slot):
        for k in range(TOP_K):
            def issue(r, carry):
                _row_copy(y_hbm, dest_ref[k * n_tok + tile * tm + r], ybuf, slot, k * tm + r, sem).start()
                return carry
            lax.fori_loop(0, tm, issue, 0, unroll=DMA_UNROLL)

    @pl.when(i == 0)
    def _():
        start_gather(0, 0)

    @pl.when(i + 1 < n_tiles)
    def _():
        start_gather(i + 1, (i + 1) % 2)

    slot = i % 2

    def wait_row(r, carry):
        _row_copy(y_hbm, 0, ybuf, slot, r, sem).wait()
        return carry
    lax.fori_loop(0, TOP_K * tm, wait_row, 0, unroll=DMA_UNROLL)

    gws = [jnp.broadcast_to(gw_ref[:, k:k + 1], (tm, LANES)) for k in range(TOP_K)]
    for s in range(ROW_CHUNKS):
        cols = slice(s * LANES, (s + 1) * LANES)
        moe = ybuf[slot, pl.ds(s, tm, stride=ROW_CHUNKS), :] * gws[0]
        for k in range(1, TOP_K):
            moe = moe + ybuf[slot, pl.ds(k * tm * ROW_CHUNKS + s, tm, stride=ROW_CHUNKS), :] * gws[k]
        o_ref[:, cols] = x_ref[:, cols] + g_ref[0][:, cols] * moe


def _combine(x2, g, y_pad, dest, gw_t, seq):
    t, d = x2.shape
    tps = seq // TOKEN_TILE
    tok = lambda i, dest_ref: (i, 0)
    return pl.pallas_call(
        _combine_kernel,
        grid_spec=pltpu.PrefetchScalarGridSpec(
            num_scalar_prefetch=1,
            grid=(t // TOKEN_TILE,),
            in_specs=[pl.BlockSpec((TOKEN_TILE, d), tok),
                      pl.BlockSpec((1, 1, d), lambda i, dest_ref: (i // tps, 0, 0)),
                      pl.BlockSpec((TOKEN_TILE, TOP_K), tok),
                      pl.BlockSpec(memory_space=pl.ANY)],
            out_specs=pl.BlockSpec((TOKEN_TILE, d), tok),
            scratch_shapes=[pltpu.VMEM((2, TOP_K * TOKEN_TILE * ROW_CHUNKS, LANES), F32),
                            pltpu.SemaphoreType.DMA((2,))]),
        out_shape=jax.ShapeDtypeStruct((t, d), F32),
        compiler_params=_params("arbitrary"),
        name="moe_combine",
    )(dest.reshape(-1), x2, g, gw_t, y_pad)


def _moe(x2, gain, sc, sh, g, router_w, router_b, w1, w3, w2, layer, seq):
    t, d = x2.shape
    n_exp = router_w.shape[1]
    h, e_idx, gw, rank, cnt = _route(x2, gain, sc, sh, router_w, router_b, seq)
    counts = cnt[:, 0].astype(I32)
    padded = (counts + EXPERT_ROWS - 1) // EXPERT_ROWS * EXPERT_ROWS
    pend = jnp.cumsum(padded)
    pstart = pend - padded
    expert_ids = jnp.arange(n_exp, dtype=I32)[:, None, None]
    dest = jnp.sum(jnp.where(e_idx[None] == expert_ids, pstart[:, None, None], 0), axis=0) + rank
    n_blocks = -(-(t * TOP_K) // EXPERT_ROWS) + n_exp
    p_len = n_blocks * EXPERT_ROWS
    blk_start = jnp.arange(n_blocks, dtype=I32) * EXPERT_ROWS
    blk_e = jnp.minimum(jnp.sum((pend[None, :] <= blk_start[:, None]).astype(I32), axis=1), n_exp - 1)
    n_active = (pend[-1:] // EXPERT_ROWS).astype(I32)
    tok_ids = jnp.broadcast_to(jnp.arange(t, dtype=I32)[None, :], dest.shape)
    src = jnp.zeros((p_len,), I32).at[dest.reshape(-1)].set(tok_ids.reshape(-1), unique_indices=True)
    y_pad = _experts(h, src, blk_e, n_active, w1, w3, w2, layer)
    return _combine(x2, g, y_pad, dest, gw.T, seq)


def _block_diag(w):
    n, a, b = w.shape
    eye = jnp.eye(n, dtype=w.dtype)
    return (eye[:, None, :, None] * w[:, :, None, :]).reshape(n * a, n * b)


def kernel(x, c, router_w, router_b, ada_w, ada_b, norm_mix, norm_ffn, ev_w_in, ev_conv_w, ev_conv_b, ev_r_w, ev_r_b, ev_i_w, ev_i_b, ev_lam, ev_forget_b, ev_q_gain, ev_k_gain, ev_w_out, od_w_in, od_gate_w2, od_gate_b, od_out_gain, od_w_out, moe_w1, moe_w3, moe_w2):
    batch, seq, d = x.shape
    depth = ada_w.shape[0]
    x2 = x.reshape(batch * seq, d)
    mods = _ada(c, ada_w, ada_b)

    for layer in range(depth):
        sh_m, sc_m, g_m, sh_f, sc_f, g_f = [mods[layer, :, None, k * d:(k + 1) * d] for k in range(6)]
        j = layer // 2
        if layer % 2 == 0:
            d_rnn = ev_conv_w.shape[-1]
            heads, head_dim = ev_forget_b.shape[-1], ev_q_gain.shape[-1]
            d_fox = heads * head_dim
            n_main = 2 * d_rnn + 3 * d_fox
            w_in = ev_w_in[j]
            wx = jnp.zeros((d, LANES), F32).at[:, :heads].set(w_in[:, n_main:]).astype(BF16)
            proj, fx = _in_proj(x2, norm_mix[layer][None, :], sc_m, sh_m, w_in[:, :n_main].astype(BF16), wx, seq)
            wg = jnp.concatenate([_block_diag(ev_r_w[j]), _block_diag(ev_i_w[j])], axis=1).astype(BF16)
            bg = jnp.concatenate([ev_r_b[j].reshape(-1), ev_i_b[j].reshape(-1)])[None, :]
            y_rnn = _rglru(proj, ev_conv_w[j], ev_conv_b[j][None, :], wg, bg, ev_lam[j][None, :], batch, seq)
            qt, kn, vt, f = _fox_prep(proj, fx, ev_forget_b[j], ev_q_gain[j], ev_k_gain[j],
                                      batch, seq, heads, head_dim)
            y_fox = _fox_attn(qt, kn, vt, f, batch, seq, heads, head_dim)
            w_out = ev_w_out[j].astype(BF16)
            x2 = _out_proj([y_rnn, y_fox], [w_out[:d_rnn], w_out[d_rnn:]], x2, g_m, seq)
        else:
            rank, d_k = od_gate_w2.shape[1:]
            d_v = od_w_out.shape[1]
            heads = d_v // od_out_gain.shape[-1]
            n_main = 2 * d_k + 2 * d_v
            w_in = od_w_in[j]
            wx = jnp.zeros((d, LANES), F32).at[:, :rank].set(w_in[:, n_main:]).astype(BF16)
            proj, gx = _in_proj(x2, norm_mix[layer][None, :], sc_m, sh_m, w_in[:, :n_main].astype(BF16), wx, seq)
            y = _gla(proj, gx, od_gate_w2[j], od_gate_b[j], od_out_gain[j], batch, seq, heads)
            x2 = _out_proj([y], [od_w_out[j].astype(BF16)], x2, g_m, seq)
        x2 = _moe(x2, norm_ffn[layer][None, :], sc_f, sh_f, g_f, router_w, router_b,
                  moe_w1, moe_w3, moe_w2, layer, seq)
    return x2.reshape(batch, seq, d)
```

```python
import functools

import jax
import jax.numpy as jnp
from jax import lax
from jax.experimental import pallas as pl
from jax.experimental.pallas import tpu as pltpu

F32 = jnp.float32
BF16 = jnp.bfloat16
I32 = jnp.int32
HIGHEST = lax.Precision.HIGHEST

EPS = 1e-6
RGLRU_C = 8.0
CONV_W = 4
GATE_TAU = 16.0
GLA_CHUNK = 64
N_GROUPS = 4
TOP_K = 2
NEG_BIG = -1e30
LOG2E = 1.4426950408889634

LANES = 128
SUBLANES = 8
TOKEN_TILE = 512
SEQ_TILE = 256
ATTN_TILE = 256
EXPERT_ROWS = 256
DMA_UNROLL = 16
ROW_CHUNKS = 8
VMEM_LIMIT = 48 * 1024 * 1024


def _params(*sem):
    return pltpu.CompilerParams(dimension_semantics=sem, vmem_limit_bytes=VMEM_LIMIT)


def _rms_mod(x, gain, sc, sh):
    ms = jnp.mean(x * x, axis=-1, keepdims=True)
    return (x * lax.rsqrt(ms + EPS) * gain) * (1.0 + sc) + sh


def _softplus(z):
    return jnp.maximum(z, 0.0) + jnp.log1p(jnp.exp(-jnp.abs(z)))


def _sigmoid(z):
    return 1.0 / (1.0 + jnp.exp(-z))


def _shift_rows(v, d, fill, row):
    return jnp.where(row >= d, pltpu.roll(v, d, axis=0), fill)


def _cumsum_rows(v):
    n = v.shape[0]
    row = lax.broadcasted_iota(I32, v.shape, 0)
    d = 1
    while d < n:
        v = v + _shift_rows(v, d, 0.0, row)
        d *= 2
    return v


def _ada_kernel(c_ref, w_ref, b_ref, o_ref):
    c = c_ref[...]
    o_ref[0] = jnp.dot(c * _sigmoid(c), w_ref[0], preferred_element_type=F32, precision=HIGHEST) + b_ref[0]


def _ada(c, ada_w, ada_b):
    depth, d, n = ada_w.shape
    b = c.shape[0]
    tn = n // 4
    return pl.pallas_call(
        _ada_kernel,
        grid=(depth, n // tn),
        in_specs=[pl.BlockSpec((b, d), lambda l, j: (0, 0)),
                  pl.BlockSpec((1, d, tn), lambda l, j: (l, 0, j)),
                  pl.BlockSpec((1, 1, tn), lambda l, j: (l, 0, j))],
        out_specs=pl.BlockSpec((1, b, tn), lambda l, j: (l, 0, j)),
        out_shape=jax.ShapeDtypeStruct((depth, b, n), F32),
        compiler_params=_params("parallel", "parallel"),
        name="ada_mod",
    )(c, ada_w, ada_b.reshape(depth, 1, n))


def _in_proj_kernel(x_ref, gain_ref, sc_ref, sh_ref, w_ref, wx_ref, o_ref, ox_ref, *, n_chunk):
    h = _rms_mod(x_ref[...], gain_ref[...], sc_ref[0], sh_ref[0]).astype(BF16)
    for c in range(0, o_ref.shape[1], n_chunk):
        o_ref[:, c:c + n_chunk] = jnp.dot(h, w_ref[:, c:c + n_chunk],
                                          preferred_element_type=F32).astype(o_ref.dtype)
    ox_ref[...] = jnp.dot(h, wx_ref[...], preferred_element_type=F32)


def _in_proj(x2, gain, sc, sh, w, wx, seq):
    t, d = x2.shape
    n = w.shape[1]
    tps = seq // TOKEN_TILE
    return pl.pallas_call(
        functools.partial(_in_proj_kernel, n_chunk=512),
        grid=(t // TOKEN_TILE,),
        in_specs=[pl.BlockSpec((TOKEN_TILE, d), lambda i: (i, 0)),
                  pl.BlockSpec((1, d), lambda i: (0, 0)),
                  pl.BlockSpec((1, 1, d), lambda i: (i // tps, 0, 0)),
                  pl.BlockSpec((1, 1, d), lambda i: (i // tps, 0, 0)),
                  pl.BlockSpec((d, n), lambda i: (0, 0)),
                  pl.BlockSpec((d, LANES), lambda i: (0, 0))],
        out_specs=[pl.BlockSpec((TOKEN_TILE, n), lambda i: (i, 0)),
                   pl.BlockSpec((TOKEN_TILE, LANES), lambda i: (i, 0))],
        out_shape=[jax.ShapeDtypeStruct((t, n), BF16), jax.ShapeDtypeStruct((t, LANES), F32)],
        compiler_params=_params("parallel"),
        name="in_proj",
    )(x2, gain, sc, sh, w, wx)


def _out_proj_kernel(*refs, n_parts):
    y_refs, w_refs = refs[:n_parts], refs[n_parts:2 * n_parts]
    x_ref, g_ref, o_ref = refs[2 * n_parts:]
    acc = jnp.dot(y_refs[0][...], w_refs[0][...], preferred_element_type=F32)
    for y_ref, w_ref in zip(y_refs[1:], w_refs[1:]):
        acc = acc + jnp.dot(y_ref[...], w_ref[...], preferred_element_type=F32)
    o_ref[...] = x_ref[...] + g_ref[0] * acc


def _out_proj(ys, ws, x2, g, seq):
    t, d = x2.shape
    tps = seq // TOKEN_TILE
    n_parts = len(ys)
    in_specs = [pl.BlockSpec((TOKEN_TILE, y.shape[1]), lambda i: (i, 0)) for y in ys]
    in_specs += [pl.BlockSpec(w.shape, lambda i: (0, 0)) for w in ws]
    in_specs += [pl.BlockSpec((TOKEN_TILE, d), lambda i: (i, 0)),
                 pl.BlockSpec((1, 1, d), lambda i: (i // tps, 0, 0))]
    return pl.pallas_call(
        functools.partial(_out_proj_kernel, n_parts=n_parts),
        grid=(t // TOKEN_TILE,),
        in_specs=in_specs,
        out_specs=pl.BlockSpec((TOKEN_TILE, d), lambda i: (i, 0)),
        out_shape=jax.ShapeDtypeStruct((t, d), F32),
        compiler_params=_params("parallel"),
        name="out_proj",
    )(*ys, *ws, x2, g)


def _rglru_kernel(x_ref, gate_ref, cw_ref, cb_ref, wg_ref, bg_ref, lam_ref, y_ref, ext_ref, hc_ref):
    rows, ch = x_ref.shape

    @pl.when(pl.program_id(1) == 0)
    def _():
        ext_ref[0:SUBLANES, :] = jnp.zeros((SUBLANES, ch), F32)
        hc_ref[...] = jnp.zeros_like(hc_ref)

    x = x_ref[...].astype(F32)
    ext_ref[SUBLANES:SUBLANES + rows, :] = x
    xc = cb_ref[...] + cw_ref[CONV_W - 1:CONV_W, :] * x
    for j in range(CONV_W - 1):
        off = SUBLANES - (CONV_W - 1) + j
        xc = xc + cw_ref[j:j + 1, :] * ext_ref[off:off + rows, :]
    ext_ref[0:SUBLANES, :] = x[rows - SUBLANES:, :]

    gates = jnp.dot(xc.astype(BF16), wg_ref[...], preferred_element_type=F32) + bg_ref[...]
    r = _sigmoid(gates[:, :ch])
    gi = _sigmoid(gates[:, ch:])
    log_a = (-RGLRU_C * _softplus(-lam_ref[...])) * r
    a = jnp.exp(log_a)
    u = jnp.sqrt(1.0 - a * a) * (gi * xc)

    row = lax.broadcasted_iota(I32, (rows, ch), 0)
    d = 1
    while d < rows:
        u = u + a * _shift_rows(u, d, 0.0, row)
        a = a * _shift_rows(a, d, 1.0, row)
        d *= 2
    h = u + a * hc_ref[...]
    hc_ref[...] = h[rows - 1:rows, :]
    y_ref[...] = (h * jax.nn.gelu(gate_ref[...].astype(F32))).astype(y_ref.dtype)


def _rglru(proj, conv_w, conv_b, wg, bg, lam, batch, seq):
    t = proj.shape[0]
    ch = conv_w.shape[1]
    nt = seq // SEQ_TILE
    return pl.pallas_call(
        _rglru_kernel,
        grid=(batch, nt),
        in_specs=[pl.BlockSpec((SEQ_TILE, ch), lambda b, j: (b * nt + j, 0)),
                  pl.BlockSpec((SEQ_TILE, ch), lambda b, j: (b * nt + j, 1)),
                  pl.BlockSpec((CONV_W, ch), lambda b, j: (0, 0)),
                  pl.BlockSpec((1, ch), lambda b, j: (0, 0)),
                  pl.BlockSpec((ch, 2 * ch), lambda b, j: (0, 0)),
                  pl.BlockSpec((1, 2 * ch), lambda b, j: (0, 0)),
                  pl.BlockSpec((1, ch), lambda b, j: (0, 0))],
        out_specs=pl.BlockSpec((SEQ_TILE, ch), lambda b, j: (b * nt + j, 0)),
        out_shape=jax.ShapeDtypeStruct((t, ch), BF16),
        scratch_shapes=[pltpu.VMEM((SEQ_TILE + SUBLANES, ch), F32), pltpu.VMEM((1, ch), F32)],
        compiler_params=_params("parallel", "arbitrary"),
        name="rglru",
    )(proj, proj, conv_w, conv_b, wg, bg, lam)


def _fox_prep_kernel(q_ref, k_ref, v_ref, fx_ref, fb_ref, qg_ref, kg_ref, ones_ref, qt_ref, kn_ref, vt_ref, f_ref,
                     fc_ref, *, scale):
    @pl.when(pl.program_id(1) == 0)
    def _():
        fc_ref[...] = jnp.zeros_like(fc_ref)

    def qk_norm(v_ref, g_ref, mult):
        v = v_ref[...].astype(F32)
        ms = jnp.dot((v * v).astype(BF16), ones_ref[...], preferred_element_type=F32)
        return (v * lax.rsqrt(ms + EPS) * g_ref[...]) * mult

    qt_ref[0, 0] = qk_norm(q_ref, qg_ref, scale * LOG2E).T.astype(qt_ref.dtype)
    kn_ref[...] = qk_norm(k_ref, kg_ref, 1.0).astype(kn_ref.dtype)
    vt_ref[0, 0] = v_ref[...].astype(F32).T.astype(vt_ref.dtype)

    log_f = -_softplus(-(fx_ref[...] + fb_ref[...])) * LOG2E
    f = _cumsum_rows(log_f) + fc_ref[...]
    fc_ref[...] = f[f.shape[0] - 1:, :]
    f_ref[...] = f


def _fox_prep(proj, fx, forget_b, q_gain, k_gain, batch, seq, heads, head_dim):
    t = proj.shape[0]
    dq = heads * head_dim
    nt = seq // ATTN_TILE
    qblk = (proj.shape[1] - 3 * dq) // dq
    fb = jnp.zeros((1, LANES), F32).at[0, :heads].set(forget_b)
    qg = jnp.tile(q_gain, heads)[None, :]
    kg = jnp.tile(k_gain, heads)[None, :]
    head_of = jnp.arange(dq) // head_dim
    ones = (head_of[:, None] == head_of[None, :]).astype(BF16) * (1.0 / head_dim)
    tok = lambda b, j: (b * nt + j, 0)
    return pl.pallas_call(
        functools.partial(_fox_prep_kernel, scale=head_dim ** -0.5),
        grid=(batch, nt),
        in_specs=[pl.BlockSpec((ATTN_TILE, dq), lambda b, j: (b * nt + j, qblk)),
                  pl.BlockSpec((ATTN_TILE, dq), lambda b, j: (b * nt + j, qblk + 1)),
                  pl.BlockSpec((ATTN_TILE, dq), lambda b, j: (b * nt + j, qblk + 2)),
                  pl.BlockSpec((ATTN_TILE, LANES), tok),
                  pl.BlockSpec((1, LANES), lambda b, j: (0, 0)),
                  pl.BlockSpec((1, dq), lambda b, j: (0, 0)),
                  pl.BlockSpec((1, dq), lambda b, j: (0, 0)),
                  pl.BlockSpec((dq, dq), lambda b, j: (0, 0))],
        out_specs=[pl.BlockSpec((1, 1, dq, ATTN_TILE), lambda b, j: (b, j, 0, 0)),
                   pl.BlockSpec((ATTN_TILE, dq), tok),
                   pl.BlockSpec((1, 1, dq, ATTN_TILE), lambda b, j: (b, j, 0, 0)),
                   pl.BlockSpec((ATTN_TILE, LANES), tok)],
        out_shape=[jax.ShapeDtypeStruct((batch, nt, dq, ATTN_TILE), BF16),
                   jax.ShapeDtypeStruct((t, dq), BF16),
                   jax.ShapeDtypeStruct((batch, nt, dq, ATTN_TILE), BF16),
                   jax.ShapeDtypeStruct((t, LANES), F32)],
        scratch_shapes=[pltpu.VMEM((1, LANES), F32)],
        compiler_params=_params("parallel", "arbitrary"),
        name="fox_prep",
    )(proj, proj, proj, fx, fb, qg, kg, ones)


def _fox_attn_kernel(qt_ref, k_ref, vt_ref, f_ref, y_ref, qm_ref, lg_ref, m_ref, l_ref, acc_ref, *,
                     heads, head_dim):
    tq = y_ref.shape[0]
    i = pl.program_id(1)
    pair = LANES // head_dim
    feat = lax.broadcasted_iota(I32, (LANES, tq), 0)
    key_i = lax.broadcasted_iota(I32, (tq, tq), 0)
    qry_i = lax.broadcasted_iota(I32, (tq, tq), 1)

    for h in range(heads):
        p, s = divmod(h, pair)
        q2t = qt_ref[0, 0, p * LANES:(p + 1) * LANES, :]
        qm_ref[h] = jnp.where((feat >= s * head_dim) & (feat < (s + 1) * head_dim), q2t, jnp.zeros_like(q2t))
    m_ref[...] = jnp.full(m_ref.shape, NEG_BIG, F32)
    l_ref[...] = jnp.zeros(l_ref.shape, F32)
    acc_ref[...] = jnp.zeros(acc_ref.shape, F32)

    def tile(j, masked):
        start = pl.multiple_of(j * tq, tq)
        fk = f_ref[pl.ds(start, tq), :]
        for h in range(heads):
            p = h // pair
            k2 = k_ref[pl.ds(start, tq), p * LANES:(p + 1) * LANES]
            lg_ref[h] = jnp.dot(k2, qm_ref[h], preferred_element_type=F32)
        for h in range(heads):
            vt = vt_ref[0, j, h * head_dim:(h + 1) * head_dim, :]
            logit = lg_ref[h] - fk[:, h:h + 1]
            if masked:
                logit = jnp.where(key_i <= qry_i, logit, NEG_BIG)
            m_old = m_ref[h][0:1, :]
            m_new = jnp.maximum(m_old, jnp.max(logit, axis=0, keepdims=True))
            alpha = jnp.exp2(m_old - m_new)
            pr = jnp.exp2(logit - m_new)
            l_new = alpha * l_ref[h][0:1, :] + jnp.sum(pr, axis=0, keepdims=True)
            acc_ref[h] = alpha * acc_ref[h] + jnp.dot(vt, pr.astype(BF16), preferred_element_type=F32)
            m_ref[h] = jnp.broadcast_to(m_new, (SUBLANES, tq))
            l_ref[h] = jnp.broadcast_to(l_new, (SUBLANES, tq))

    def body(j, carry):
        tile(j, False)
        return carry

    lax.fori_loop(0, i, body, 0)
    tile(i, True)

    for p in range(heads // pair):
        out_t = jnp.concatenate([acc_ref[h] / l_ref[h][0:1, :] for h in range(p * pair, (p + 1) * pair)], axis=0)
        y_ref[:, p * LANES:(p + 1) * LANES] = out_t.T.astype(y_ref.dtype)


def _fox_attn(qt, kn, vt, f, batch, seq, heads, head_dim):
    t, dq = kn.shape
    nq = seq // ATTN_TILE
    return pl.pallas_call(
        functools.partial(_fox_attn_kernel, heads=heads, head_dim=head_dim),
        grid=(batch, nq),
        in_specs=[pl.BlockSpec((1, 1, dq, ATTN_TILE), lambda b, i: (b, i, 0, 0)),
                  pl.BlockSpec((seq, dq), lambda b, i: (b, 0)),
                  pl.BlockSpec((1, nq, dq, ATTN_TILE), lambda b, i: (b, 0, 0, 0)),
                  pl.BlockSpec((seq, LANES), lambda b, i: (b, 0))],
        out_specs=pl.BlockSpec((ATTN_TILE, dq), lambda b, i: (b * nq + i, 0)),
        out_shape=jax.ShapeDtypeStruct((t, dq), BF16),
        scratch_shapes=[pltpu.VMEM((heads, LANES, ATTN_TILE), BF16),
                        pltpu.VMEM((heads, ATTN_TILE, ATTN_TILE), F32),
                        pltpu.VMEM((heads, SUBLANES, ATTN_TILE), F32),
                        pltpu.VMEM((heads, SUBLANES, ATTN_TILE), F32),
                        pltpu.VMEM((heads, head_dim, ATTN_TILE), F32)],
        compiler_params=_params("parallel", "arbitrary"),
        name="fox_attn",
    )(qt, kn, vt, f)


def _gla_kernel(q_ref, k_ref, v_ref, r_ref, gx_ref, w2_ref, gb_ref, og_ref, y_ref, st_ref, *, heads, scale):
    rows = q_ref.shape[0]
    dk = q_ref.shape[1] // heads
    dv = v_ref.shape[1] // heads
    ck = GLA_CHUNK

    @pl.when(pl.program_id(1) == 0)
    def _():
        st_ref[...] = jnp.zeros_like(st_ref)

    g = jnp.dot(gx_ref[...].astype(BF16), w2_ref[...], preferred_element_type=F32) + gb_ref[...]
    log_g = -_softplus(-g) * (1.0 / GATE_TAU)
    rowc = lax.broadcasted_iota(I32, (ck, ck), 0)
    colc = lax.broadcasted_iota(I32, (ck, ck), 1)

    for c in range(rows // ck):
        rs = slice(c * ck, (c + 1) * ck)
        for h in range(heads):
            ks = slice(h * dk, (h + 1) * dk)
            vs = slice(h * dv, (h + 1) * dv)
            cum = _cumsum_rows(log_g[rs, ks])
            last = cum[ck - 1:ck, :]
            kf = k_ref[rs, ks].astype(F32)
            q_b = (q_ref[rs, ks].astype(F32) * scale * jnp.exp(cum)).astype(BF16)
            k_b = (kf * jnp.exp(-cum)).astype(BF16)
            ke_b = (kf * jnp.exp(last - cum)).astype(BF16)
            v = v_ref[rs, vs]
            intra = lax.dot_general(q_b, k_b, (((1,), (1,)), ((), ())), preferred_element_type=F32)
            intra = jnp.where(colc <= rowc, intra, 0.0)
            state = st_ref[h]
            o = (jnp.dot(intra.astype(BF16), v, preferred_element_type=F32)
                 + lax.dot_general(q_b, state.astype(BF16), (((1,), (1,)), ((), ())),
                                   preferred_element_type=F32))
            st_ref[h] = state * jnp.exp(last) + lax.dot_general(
                v, ke_b, (((0,), (0,)), ((), ())), preferred_element_type=F32)
            ms = jnp.mean(o * o, axis=-1, keepdims=True)
            o = o * lax.rsqrt(ms + EPS) * og_ref[...]
            rg = r_ref[rs, vs].astype(F32)
            y_ref[rs, vs] = (o * (rg * _sigmoid(rg))).astype(y_ref.dtype)


def _gla(proj, gx, gate_w2, gate_b, out_gain, batch, seq, heads):
    t = proj.shape[0]
    rank, d_k = gate_w2.shape
    d_v = heads * out_gain.shape[0]
    nt = seq // SEQ_TILE
    kb = d_k // d_k
    w2 = jnp.zeros((LANES, d_k), F32).at[:rank].set(gate_w2).astype(BF16)
    tok = lambda b, j: (b * nt + j, 0)
    vblk = 2 * d_k // d_v
    return pl.pallas_call(
        functools.partial(_gla_kernel, heads=heads, scale=(d_k // heads) ** -0.5),
        grid=(batch, nt),
        in_specs=[pl.BlockSpec((SEQ_TILE, d_k), lambda b, j: (b * nt + j, 0)),
                  pl.BlockSpec((SEQ_TILE, d_k), lambda b, j: (b * nt + j, kb)),
                  pl.BlockSpec((SEQ_TILE, d_v), lambda b, j: (b * nt + j, vblk)),
                  pl.BlockSpec((SEQ_TILE, d_v), lambda b, j: (b * nt + j, vblk + 1)),
                  pl.BlockSpec((SEQ_TILE, LANES), tok),
                  pl.BlockSpec((LANES, d_k), lambda b, j: (0, 0)),
                  pl.BlockSpec((1, d_k), lambda b, j: (0, 0)),
                  pl.BlockSpec((1, d_v // heads), lambda b, j: (0, 0))],
        out_specs=pl.BlockSpec((SEQ_TILE, d_v), tok),
        out_shape=jax.ShapeDtypeStruct((t, d_v), BF16),
        scratch_shapes=[pltpu.VMEM((heads, d_v // heads, d_k // heads), F32)],
        compiler_params=_params("parallel", "arbitrary"),
        name="gla",
    )(proj, proj, proj, proj, gx, w2, gate_b[None, :], out_gain[None, :])


def _route_kernel(x_ref, gain_ref, sc_ref, sh_ref, rw_ref, rb_ref, tri_ref,
                  h_ref, eidx_ref, gw_ref, rank_ref, cnt_ref, *, n_groups):
    n_exp = rw_ref.shape[0]
    epg = n_exp // n_groups
    tm = x_ref.shape[0]

    @pl.when(pl.program_id(0) == 0)
    def _():
        cnt_ref[...] = jnp.zeros_like(cnt_ref)

    h = _rms_mod(x_ref[...], gain_ref[...], sc_ref[0], sh_ref[0])
    nch = h.shape[1] // LANES
    for s in range(nch):
        h_ref[pl.ds(s, tm, stride=nch), :] = h[:, s * LANES:(s + 1) * LANES]
    logits = lax.dot_general(rw_ref[...], h, (((1,), (1,)), ((), ())),
                             preferred_element_type=F32, precision=HIGHEST)
    scores = _sigmoid(logits)
    sel = scores + rb_ref[...]

    sub = lax.broadcasted_iota(I32, (epg, tm), 0)

    def top2(v):
        m1 = jnp.max(v, axis=0, keepdims=True)
        i1 = jnp.min(jnp.where(v == m1, sub, epg), axis=0, keepdims=True)
        v2 = jnp.where(sub == i1, -jnp.inf, v)
        m2 = jnp.max(v2, axis=0, keepdims=True)
        i2 = jnp.min(jnp.where(v2 == m2, sub, epg), axis=0, keepdims=True)
        return m1, i1, m2, i2

    groups = [sel[g * epg:(g + 1) * epg, :] for g in range(n_groups)]
    best = None
    for g in range(n_groups):
        m1, _, m2, _ = top2(groups[g])
        gs = m1 + m2
        if g == 0:
            best, g_idx = gs, jnp.zeros((1, tm), I32)
        else:
            better = gs > best
            g_idx = jnp.where(better, g, g_idx)
            best = jnp.where(better, gs, best)
    in_group = groups[0]
    sc_group = scores[0:epg, :]
    for g in range(1, n_groups):
        in_group = jnp.where(g_idx == g, groups[g], in_group)
        sc_group = jnp.where(g_idx == g, scores[g * epg:(g + 1) * epg, :], sc_group)
    _, i1, _, i2 = top2(in_group)
    w0 = jnp.sum(jnp.where(sub == i1, sc_group, 0.0), axis=0, keepdims=True)
    w1 = jnp.sum(jnp.where(sub == i2, sc_group, 0.0), axis=0, keepdims=True)
    wsum = w0 + w1
    e0 = g_idx * epg + i1
    e1 = g_idx * epg + i2
    eidx_ref[0:1, :] = e0
    eidx_ref[1:2, :] = e1
    gw_ref[0:1, :] = w0 / wsum
    gw_ref[1:2, :] = w1 / wsum

    erow = lax.broadcasted_iota(I32, (n_exp, tm), 0)
    oh0 = erow == e0
    oh1 = erow == e1
    taken = jnp.where(oh0 | oh1, 1.0, 0.0)
    before = jnp.dot(taken.astype(BF16), tri_ref[...], preferred_element_type=F32) + cnt_ref[:, 0:1]
    r0 = jnp.sum(jnp.where(oh0, before, 0.0), axis=0, keepdims=True)
    r1 = jnp.sum(jnp.where(oh1, before, 0.0), axis=0, keepdims=True)
    rank_ref[0:1, :] = r0.astype(I32)
    rank_ref[1:2, :] = r1.astype(I32)
    cnt_ref[...] = cnt_ref[...] + jnp.sum(taken, axis=1, keepdims=True)


def _route(x2, gain, sc, sh, router_w, router_b, seq):
    t, d = x2.shape
    n_exp = router_w.shape[1]
    tps = seq // TOKEN_TILE
    pos = jnp.arange(TOKEN_TILE)
    tri = (pos[:, None] < pos[None, :]).astype(BF16)
    tokl = lambda i: (0, i)
    return pl.pallas_call(
        functools.partial(_route_kernel, n_groups=N_GROUPS),
        grid=(t // TOKEN_TILE,),
        in_specs=[pl.BlockSpec((TOKEN_TILE, d), lambda i: (i, 0)),
                  pl.BlockSpec((1, d), lambda i: (0, 0)),
                  pl.BlockSpec((1, 1, d), lambda i: (i // tps, 0, 0)),
                  pl.BlockSpec((1, 1, d), lambda i: (i // tps, 0, 0)),
                  pl.BlockSpec((n_exp, d), lambda i: (0, 0)),
                  pl.BlockSpec((n_exp, 1), lambda i: (0, 0)),
                  pl.BlockSpec((TOKEN_TILE, TOKEN_TILE), lambda i: (0, 0))],
        out_specs=[pl.BlockSpec((TOKEN_TILE * (d // LANES), LANES), lambda i: (i, 0)),
                   pl.BlockSpec((TOP_K, TOKEN_TILE), tokl),
                   pl.BlockSpec((TOP_K, TOKEN_TILE), tokl),
                   pl.BlockSpec((TOP_K, TOKEN_TILE), tokl),
                   pl.BlockSpec((n_exp, LANES), lambda i: (0, 0))],
        out_shape=[jax.ShapeDtypeStruct((t * (d // LANES), LANES), F32),
                   jax.ShapeDtypeStruct((TOP_K, t), I32),
                   jax.ShapeDtypeStruct((TOP_K, t), F32),
                   jax.ShapeDtypeStruct((TOP_K, t), I32),
                   jax.ShapeDtypeStruct((n_exp, LANES), F32)],
        compiler_params=_params("arbitrary"),
        name="moe_route",
    )(x2, gain, sc, sh, router_w.T, router_b[:, None], tri)


def _row_copy(table_hbm, row, buf, slot, r, sem):
    src = table_hbm.at[pl.ds(pl.multiple_of(row * ROW_CHUNKS, ROW_CHUNKS), ROW_CHUNKS), :]
    dst = buf.at[slot, pl.ds(pl.multiple_of(r * ROW_CHUNKS, ROW_CHUNKS), ROW_CHUNKS), :]
    return pltpu.make_async_copy(src, dst, sem.at[slot])


def _expert_kernel(blk_e_ref, nact_ref, src_ref, h_hbm, w1_ref, w3_ref, w2_ref, o_ref,
                   xbuf, sem, xrow, w1b, w3b, w2b):
    b = pl.program_id(0)
    n_act = nact_ref[0]
    rows = xbuf.shape[1] // ROW_CHUNKS

    def start_gather(blk, slot):
        def issue(r, carry):
            _row_copy(h_hbm, src_ref[blk * rows + r], xbuf, slot, r, sem).start()
            return carry
        lax.fori_loop(0, rows, issue, 0, unroll=DMA_UNROLL)

    @pl.when(b == 0)
    def _():
        start_gather(0, 0)

    @pl.when(b + 1 < n_act)
    def _():
        start_gather(b + 1, (b + 1) % 2)

    @pl.when(b < n_act)
    def _():
        slot = b % 2

        def wait_row(r, carry):
            _row_copy(h_hbm, 0, xbuf, slot, r, sem).wait()
            return carry
        lax.fori_loop(0, rows, wait_row, 0, unroll=DMA_UNROLL)

        changed = blk_e_ref[b] != blk_e_ref[jnp.maximum(b - 1, 0)]

        @pl.when((b == 0) | changed)
        def _():
            w1b[...] = w1_ref[0, 0].astype(BF16)
            w3b[...] = w3_ref[0, 0].astype(BF16)
            w2b[...] = w2_ref[0, 0].astype(BF16)

        for s in range(ROW_CHUNKS):
            xrow[:, s * LANES:(s + 1) * LANES] = xbuf[slot, pl.ds(s, rows, stride=ROW_CHUNKS), :].astype(BF16)
        xb = xrow[...]
        a = jnp.dot(xb, w1b[...], preferred_element_type=F32)
        hid = (a * _sigmoid(a)) * jnp.dot(xb, w3b[...], preferred_element_type=F32)
        out = jnp.dot(hid.astype(BF16), w2b[...], preferred_element_type=F32)
        for s in range(ROW_CHUNKS):
            o_ref[pl.ds(s, rows, stride=ROW_CHUNKS), :] = out[:, s * LANES:(s + 1) * LANES]

    @pl.when(b >= n_act)
    def _():
        o_ref[...] = jnp.zeros_like(o_ref)


def _experts(h, src, blk_e, n_active, w1, w3, w2, layer):
    d = ROW_CHUNKS * LANES
    p_len = src.shape[0]
    de = w1.shape[-1]
    n_blocks = p_len // EXPERT_ROWS

    def last_active(b, nact_ref):
        return jnp.maximum(jnp.minimum(b, nact_ref[0] - 1), 0)

    def row_map(b, blk_e_ref, nact_ref, src_ref):
        return (b, 0)

    def w_map(b, blk_e_ref, nact_ref, src_ref):
        return (layer, blk_e_ref[last_active(b, nact_ref)], 0, 0)

    return pl.pallas_call(
        _expert_kernel,
        grid_spec=pltpu.PrefetchScalarGridSpec(
            num_scalar_prefetch=3,
            grid=(n_blocks,),
            in_specs=[pl.BlockSpec(memory_space=pl.ANY),
                      pl.BlockSpec((1, 1, d, de), w_map),
                      pl.BlockSpec((1, 1, d, de), w_map),
                      pl.BlockSpec((1, 1, de, d), w_map)],
            out_specs=pl.BlockSpec((EXPERT_ROWS * ROW_CHUNKS, LANES), row_map),
            scratch_shapes=[pltpu.VMEM((2, EXPERT_ROWS * ROW_CHUNKS, LANES), F32), pltpu.SemaphoreType.DMA((2,)),
                            pltpu.VMEM((EXPERT_ROWS, d), BF16), pltpu.VMEM((d, de), BF16), pltpu.VMEM((d, de), BF16), pltpu.VMEM((de, d), BF16)]),
        out_shape=jax.ShapeDtypeStruct((p_len * ROW_CHUNKS, LANES), F32),
        compiler_params=_params("arbitrary"),
        name="moe_experts",
    )(blk_e, n_active, src, h, w1, w3, w2)


def _combine_kernel(dest_ref, x_ref, g_ref, gw_ref, y_hbm, o_ref, ybuf, sem):
    i = pl.program_id(0)
    n_tiles = pl.num_programs(0)
    tm = x_ref.shape[0]
    n_tok = n_tiles * tm

    def start_gather(tile, slot):
        for k in range(TOP_K):
            def issue(r, carry):
                _row_copy(y_hbm, dest_ref[k * n_tok + tile * tm + r], ybuf, slot, k * tm + r, sem).start()
                return carry
            lax.fori_loop(0, tm, issue, 0, unroll=DMA_UNROLL)

    @pl.when(i == 0)
    def _():
        start_gather(0, 0)

    @pl.when(i + 1 < n_tiles)
    def _():
        start_gather(i + 1, (i + 1) % 2)

    slot = i % 2

    def wait_row(r, carry):
        _row_copy(y_hbm, 0, ybuf, slot, r, sem).wait()
        return carry
    lax.fori_loop(0, TOP_K * tm, wait_row, 0, unroll=DMA_UNROLL)

    gws = [jnp.broadcast_to(gw_ref[:, k:k + 1], (tm, LANES)) for k in range(TOP_K)]
    for s in range(ROW_CHUNKS):
        cols = slice(s * LANES, (s + 1) * LANES)
        moe = ybuf[slot, pl.ds(s, tm, stride=ROW_CHUNKS), :] * gws[0]
        for k in range(1, TOP_K):
            moe = moe + ybuf[slot, pl.ds(k * tm * ROW_CHUNKS + s, tm, stride=ROW_CHUNKS), :] * gws[k]
        o_ref[:, cols] = x_ref[:, cols] + g_ref[0][:, cols] * moe


def _combine(x2, g, y_pad, dest, gw_t, seq):
    t, d = x2.shape
    tps = seq // TOKEN_TILE
    tok = lambda i, dest_ref: (i, 0)
    return pl.pallas_call(
        _combine_kernel,
        grid_spec=pltpu.PrefetchScalarGridSpec(
            num_scalar_prefetch=1,
            grid=(t // TOKEN_TILE,),
            in_specs=[pl.BlockSpec((TOKEN_TILE, d), tok),
                      pl.BlockSpec((1, 1, d), lambda i, dest_ref: (i // tps, 0, 0)),
                      pl.BlockSpec((TOKEN_TILE, TOP_K), tok),
                      pl.BlockSpec(memory_space=pl.ANY)],
            out_specs=pl.BlockSpec((TOKEN_TILE, d), tok),
            scratch_shapes=[pltpu.VMEM((2, TOP_K * TOKEN_TILE * ROW_CHUNKS, LANES), F32),
                            pltpu.SemaphoreType.DMA((2,))]),
        out_shape=jax.ShapeDtypeStruct((t, d), F32),
        compiler_params=_params("arbitrary"),
        name="moe_combine",
    )(dest.reshape(-1), x2, g, gw_t, y_pad)


def _invert_kernel(dest_ref, src_ref):
    n_tok = dest_ref.shape[0] // TOP_K

    def clear(i, carry):
        src_ref[i] = 0
        return carry
    lax.fori_loop(0, src_ref.shape[0], clear, 0, unroll=DMA_UNROLL)

    for k in range(TOP_K):
        def place(i, carry):
            src_ref[dest_ref[k * n_tok + i]] = i
            return carry
        lax.fori_loop(0, n_tok, place, 0, unroll=DMA_UNROLL)


def _invert_slots(dest_flat, p_len):
    return pl.pallas_call(
        _invert_kernel,
        in_specs=[pl.BlockSpec(memory_space=pltpu.SMEM)],
        out_specs=pl.BlockSpec(memory_space=pltpu.SMEM),
        out_shape=jax.ShapeDtypeStruct((p_len,), I32),
        name="moe_invert",
    )(dest_flat)


def _moe(x2, gain, sc, sh, g, router_w, router_b, w1, w3, w2, layer, seq):
    t, d = x2.shape
    assert d == ROW_CHUNKS * LANES
    n_exp = router_w.shape[1]
    h, e_idx, gw, rank, cnt = _route(x2, gain, sc, sh, router_w, router_b, seq)
    counts = cnt[:, 0].astype(I32)
    padded = (counts + EXPERT_ROWS - 1) // EXPERT_ROWS * EXPERT_ROWS
    pend = jnp.cumsum(padded)
    pstart = pend - padded
    expert_ids = jnp.arange(n_exp, dtype=I32)[:, None, None]
    dest = jnp.sum(jnp.where(e_idx[None] == expert_ids, pstart[:, None, None], 0), axis=0) + rank
    n_blocks = -(-(t * TOP_K) // EXPERT_ROWS) + n_exp
    p_len = n_blocks * EXPERT_ROWS
    blk_start = jnp.arange(n_blocks, dtype=I32) * EXPERT_ROWS
    blk_e = jnp.minimum(jnp.sum((pend[None, :] <= blk_start[:, None]).astype(I32), axis=1), n_exp - 1)
    n_active = (pend[-1:] // EXPERT_ROWS).astype(I32)
    src = _invert_slots(dest.reshape(-1), p_len)
    y_pad = _experts(h, src, blk_e, n_active, w1, w3, w2, layer)
    return _combine(x2, g, y_pad, dest, gw.T, seq)


def _block_diag(w):
    n, a, b = w.shape
    eye = jnp.eye(n, dtype=w.dtype)
    return (eye[:, None, :, None] * w[:, :, None, :]).reshape(n * a, n * b)


def kernel(x, c, router_w, router_b, ada_w, ada_b, norm_mix, norm_ffn, ev_w_in, ev_conv_w, ev_conv_b, ev_r_w, ev_r_b, ev_i_w, ev_i_b, ev_lam, ev_forget_b, ev_q_gain, ev_k_gain, ev_w_out, od_w_in, od_gate_w2, od_gate_b, od_out_gain, od_w_out, moe_w1, moe_w3, moe_w2):
    batch, seq, d = x.shape
    depth = ada_w.shape[0]
    x2 = x.reshape(batch * seq, d)
    mods = _ada(c, ada_w, ada_b)

    for layer in range(depth):
        sh_m, sc_m, g_m, sh_f, sc_f, g_f = [mods[layer, :, None, k * d:(k + 1) * d] for k in range(6)]
        j = layer // 2
        if layer % 2 == 0:
            d_rnn = ev_conv_w.shape[-1]
            heads, head_dim = ev_forget_b.shape[-1], ev_q_gain.shape[-1]
            d_fox = heads * head_dim
            n_main = 2 * d_rnn + 3 * d_fox
            w_in = ev_w_in[j]
            wx = jnp.zeros((d, LANES), F32).at[:, :heads].set(w_in[:, n_main:]).astype(BF16)
            proj, fx = _in_proj(x2, norm_mix[layer][None, :], sc_m, sh_m, w_in[:, :n_main].astype(BF16), wx, seq)
            wg = jnp.concatenate([_block_diag(ev_r_w[j]), _block_diag(ev_i_w[j])], axis=1).astype(BF16)
            bg = jnp.concatenate([ev_r_b[j].reshape(-1), ev_i_b[j].reshape(-1)])[None, :]
            y_rnn = _rglru(proj, ev_conv_w[j], ev_conv_b[j][None, :], wg, bg, ev_lam[j][None, :], batch, seq)
            qt, kn, vt, f = _fox_prep(proj, fx, ev_forget_b[j], ev_q_gain[j], ev_k_gain[j],
                                      batch, seq, heads, head_dim)
            y_fox = _fox_attn(qt, kn, vt, f, batch, seq, heads, head_dim)
            w_out = ev_w_out[j].astype(BF16)
            x2 = _out_proj([y_rnn, y_fox], [w_out[:d_rnn], w_out[d_rnn:]], x2, g_m, seq)
        else:
            rank, d_k = od_gate_w2.shape[1:]
            d_v = od_w_out.shape[1]
            heads = d_v // od_out_gain.shape[-1]
            n_main = 2 * d_k + 2 * d_v
            w_in = od_w_in[j]
            wx = jnp.zeros((d, LANES), F32).at[:, :rank].set(w_in[:, n_main:]).astype(BF16)
            proj, gx = _in_proj(x2, norm_mix[layer][None, :], sc_m, sh_m, w_in[:, :n_main].astype(BF16), wx, seq)
            y = _gla(proj, gx, od_gate_w2[j], od_gate_b[j], od_out_gain[j], batch, seq, heads)
            x2 = _out_proj([y], [od_w_out[j].astype(BF16)], x2, g_m, seq)
        x2 = _moe(x2, norm_ffn[layer][None, :], sc_f, sh_f, g_f, router_w, router_b,
                  moe_w1, moe_w3, moe_w2, layer, seq)
    return x2.reshape(batch, seq, d)
```
